```python
import math
import jax, jax.numpy as jnp
from jax import lax
import numpy as np

D_MODEL = 2048
BATCH = 2
SEQ = 8192
DEPTH = 2

N_MIXERS = 2
N_SSM_LAYERS = (DEPTH + 1) // 2
N_ATTN_LAYERS = DEPTH // 2

SSM_EXPAND = 2
D_INNER = SSM_EXPAND * D_MODEL
SSM_HEAD_DIM = 64
SSM_HEADS = D_INNER // SSM_HEAD_DIM
SSM_GROUPS = 8
SSM_STATE = 128
CONV_WIDTH = 4
CHUNK = 128
CONV_DIM = D_INNER + 2 * SSM_GROUPS * SSM_STATE
IN_PROJ_DIM = D_INNER + CONV_DIM + SSM_HEADS

ATTN_GROUPS = ((128, 1), (512, 4), (2048, 16))
N_ATTN_GROUPS = len(ATTN_GROUPS)
HEADS_PER_GROUP = 16
ATTN_HEAD_DIM = 64
ATTN_WIDTH = HEADS_PER_GROUP * ATTN_HEAD_DIM
QKV_DIM = 3 * N_ATTN_GROUPS * ATTN_WIDTH

D_FF = 4 * D_MODEL
EPS = 1e-5

kernel_name = "hybrid_ssd_dilated_alibi_trunk"


def rms_norm(x, g):
    xf = x.astype(jnp.float32)
    y = xf * lax.rsqrt(jnp.mean(xf * xf, axis=-1, keepdims=True) + EPS)
    return (y * g.astype(jnp.float32)).astype(x.dtype)


def causal_depthwise_conv(u, w, b):
    out = lax.conv_general_dilated(
        u, w[:, None, :].astype(u.dtype), window_strides=(1,),
        padding=[(CONV_WIDTH - 1, 0)],
        dimension_numbers=("NWC", "WIO", "NWC"),
        feature_group_count=u.shape[-1])
    return out + b.astype(u.dtype)


def segsum_exp(a):
    T = a.shape[-1]
    cs = jnp.cumsum(a, axis=-1)
    diff = cs[..., :, None] - cs[..., None, :]
    mask = jnp.tril(jnp.ones((T, T), dtype=bool))
    return jnp.exp(jnp.where(mask, diff, -jnp.inf))


def ssd_chunked(x, a, bm, cm):
    Bsz, S, H, P = x.shape
    G, N = bm.shape[2], bm.shape[3]
    hg = H // G
    nc = S // CHUNK
    x = x.reshape(Bsz, nc, CHUNK, G, hg, P)
    a = a.reshape(Bsz, nc, CHUNK, G, hg).transpose(0, 3, 4, 1, 2)
    bm = bm.reshape(Bsz, nc, CHUNK, G, N)
    cm = cm.reshape(Bsz, nc, CHUNK, G, N)
    a_cs = jnp.cumsum(a, axis=-1)

    L = segsum_exp(a)
    cb = jnp.einsum("bclgn,bcsgn->bcgls", cm, bm)
    y_diag = jnp.einsum("bcgls,bgjcls,bcsgjp->bclgjp", cb, L, x)

    decay_states = jnp.exp(a_cs[..., -1:] - a_cs)
    states = jnp.einsum("bclgn,bgjcl,bclgjp->bcgjpn", bm, decay_states, x)
    chunk_decay = jnp.exp(a_cs[..., -1])

    def step(h, inp):
        st, dec = inp
        return h * dec[..., None, None] + st, h

    h0 = jnp.zeros((Bsz, G, hg, P, N), dtype=x.dtype)
    _, prev = lax.scan(step, h0, (states.transpose(1, 0, 2, 3, 4, 5),
                                  chunk_decay.transpose(3, 0, 1, 2)))
    prev = prev.transpose(1, 0, 2, 3, 4, 5)

    y_off = jnp.einsum("bclgn,bcgjpn,bgjcl->bclgjp", cm, prev, jnp.exp(a_cs))
    return (y_diag + y_off).reshape(Bsz, S, H, P)


def mamba2_mixer(u, w_in, conv_w, conv_b, dt_bias, a_log, d_skip, norm_w, w_out):
    Bsz, S, _ = u.shape
    zxbcdt = u @ w_in
    z = zxbcdt[..., :D_INNER]
    xbc = zxbcdt[..., D_INNER:D_INNER + CONV_DIM]
    dt = zxbcdt[..., D_INNER + CONV_DIM:]
    xbc = jax.nn.silu(causal_depthwise_conv(xbc, conv_w, conv_b))
    gn = SSM_GROUPS * SSM_STATE
    xs = xbc[..., :D_INNER].reshape(Bsz, S, SSM_HEADS, SSM_HEAD_DIM).astype(jnp.float32)
    bm = xbc[..., D_INNER:D_INNER + gn].reshape(Bsz, S, SSM_GROUPS, SSM_STATE).astype(jnp.float32)
    cm = xbc[..., D_INNER + gn:].reshape(Bsz, S, SSM_GROUPS, SSM_STATE).astype(jnp.float32)
    dt = jax.nn.softplus(dt.astype(jnp.float32) + dt_bias.astype(jnp.float32))
    A = -jnp.exp(a_log.astype(jnp.float32))
    y = ssd_chunked(xs * dt[..., None], dt * A, bm, cm)
    y = y + d_skip.astype(jnp.float32)[:, None] * xs
    y = y.reshape(Bsz, S, D_INNER) * jax.nn.silu(z.astype(jnp.float32))
    yg = y.reshape(Bsz, S, SSM_GROUPS, D_INNER // SSM_GROUPS)
    yg = yg * lax.rsqrt(jnp.mean(yg * yg, axis=-1, keepdims=True) + EPS)
    y = yg.reshape(Bsz, S, D_INNER) * norm_w.astype(jnp.float32)
    return y.astype(u.dtype) @ w_out


def alibi_slopes():
    n = N_ATTN_GROUPS * HEADS_PER_GROUP
    i = jnp.arange(1, n + 1, dtype=jnp.float32)
    return jnp.exp2(-8.0 * i / n).reshape(N_ATTN_GROUPS, HEADS_PER_GROUP)


def dilated_window_group(q, k, v, window, dilation, slopes):
    Bsz, S, H, dh = q.shape
    blk = window // dilation
    span = dilation * blk
    S_pad = -(-S // span) * span
    Lsub = S_pad // dilation
    nb = Lsub // blk

    def to_blocks(t):
        t = jnp.pad(t, ((0, 0), (0, S_pad - S), (0, 0), (0, 0)))
        t = t.reshape(Bsz, Lsub, dilation, H, dh).transpose(0, 2, 1, 3, 4)
        return t.reshape(Bsz, dilation, nb, blk, H, dh)

    def with_prev(t):
        prev = jnp.pad(t[:, :, :-1], ((0, 0), (0, 0), (1, 0), (0, 0), (0, 0), (0, 0)))
        return jnp.concatenate([prev, t], axis=3)

    qb = to_blocks(q)
    kk = with_prev(to_blocks(k))
    vv = with_prev(to_blocks(v))

    s = jnp.einsum("brnqhd,brnkhd->brnhqk", qb, kk).astype(jnp.float32) * (1.0 / math.sqrt(dh))
    qi = jnp.arange(blk)[:, None]
    kj = jnp.arange(2 * blk)[None, :]
    dist = qi + blk - kj
    valid = (dist >= 0) & (dist <= blk)
    valid = valid[None] & ((jnp.arange(nb)[:, None, None] > 0) | (kj[None] >= blk))
    bias = -slopes.astype(jnp.float32)[:, None, None] * (dist * dilation).astype(jnp.float32)[None]
    s = s + bias[None, None, None]
    s = jnp.where(valid[None, None, :, None], s, -jnp.inf)
    m = jnp.max(s, axis=-1, keepdims=True)
    p = jnp.exp(s - m)
    den = jnp.sum(p, axis=-1)
    o = jnp.einsum("brnhqk,brnkhd->brnqhd", p, vv.astype(jnp.float32))
    o = o / den.transpose(0, 1, 2, 4, 3)[..., None]
    lse = (m[..., 0] + jnp.log(den)).transpose(0, 1, 2, 4, 3)

    o = o.reshape(Bsz, dilation, Lsub, H, dh).transpose(0, 2, 1, 3, 4).reshape(Bsz, S_pad, H, dh)[:, :S]
    lse = lse.reshape(Bsz, dilation, Lsub, H).transpose(0, 2, 1, 3).reshape(Bsz, S_pad, H)[:, :S]
    return o, lse


def dilated_attention(u, w_qkv, w_o):
    Bsz, S, _ = u.shape
    qkv = (u @ w_qkv).reshape(Bsz, S, 3, N_ATTN_GROUPS, HEADS_PER_GROUP, ATTN_HEAD_DIM)
    slopes = alibi_slopes()
    outs, lses = [], []
    for g, (window, dilation) in enumerate(ATTN_GROUPS):
        o, l = dilated_window_group(qkv[:, :, 0, g], qkv[:, :, 1, g], qkv[:, :, 2, g],
                                    window, dilation, slopes[g])
        outs.append(o)
        lses.append(l)
    w = jax.nn.softmax(jnp.stack(lses, axis=0), axis=0)
    o = jnp.sum(w[..., None] * jnp.stack(outs, axis=0), axis=0)
    return o.reshape(Bsz, S, ATTN_WIDTH).astype(u.dtype) @ w_o


def sq_relu_mlp(h, w1, w2):
    a = jax.nn.relu(h @ w1)
    return (a * a) @ w2


def setup_inputs(seed: int = 0) -> dict:
    key = jax.random.key(seed)
    ks = jax.random.split(key, 20)
    f32 = jnp.float32
    nS, nA, L = N_SSM_LAYERS, N_ATTN_LAYERS, DEPTH

    def normal(k, shape, scale):
        return jax.random.normal(k, shape, f32) * scale

    x = jax.random.normal(ks[0], (BATCH, SEQ, D_MODEL), f32)
    norm_mix = 1.0 + normal(ks[1], (L, D_MODEL), 0.02)
    norm_mlp = 1.0 + normal(ks[2], (L, D_MODEL), 0.02)

    ssm_w_in = normal(ks[3], (nS, D_MODEL, IN_PROJ_DIM), D_MODEL ** -0.5)
    ssm_conv_w = normal(ks[4], (nS, CONV_WIDTH, CONV_DIM), CONV_WIDTH ** -0.5)
    ssm_conv_b = normal(ks[5], (nS, CONV_DIM), 0.02)
    dt0 = jnp.exp(jax.random.uniform(ks[6], (nS, SSM_HEADS), f32,
                                     math.log(1e-3), math.log(1e-1)))
    ssm_dt_bias = dt0 + jnp.log(-jnp.expm1(-dt0))
    ssm_a_log = jnp.log(jax.random.uniform(ks[7], (nS, SSM_HEADS), f32, 1.0, 16.0))
    ssm_d = 1.0 + normal(ks[8], (nS, SSM_HEADS), 0.02)
    ssm_norm_w = 1.0 + normal(ks[9], (nS, D_INNER), 0.02)
    ssm_w_out = normal(ks[10], (nS, D_INNER, D_MODEL), D_INNER ** -0.5)

    attn_w_qkv = normal(ks[11], (nA, D_MODEL, QKV_DIM), D_MODEL ** -0.5)
    attn_w_o = normal(ks[12], (nA, ATTN_WIDTH, D_MODEL), ATTN_WIDTH ** -0.5)

    mlp_w1 = normal(ks[13], (L, D_MODEL, D_FF), D_MODEL ** -0.5)
    mlp_w2 = normal(ks[14], (L, D_FF, D_MODEL), D_FF ** -0.5)
    final_norm = 1.0 + normal(ks[15], (D_MODEL,), 0.02)
    return {"x": x, "norm_mix": norm_mix, "norm_mlp": norm_mlp,
            "ssm_w_in": ssm_w_in, "ssm_conv_w": ssm_conv_w, "ssm_conv_b": ssm_conv_b,
            "ssm_dt_bias": ssm_dt_bias, "ssm_a_log": ssm_a_log, "ssm_d": ssm_d,
            "ssm_norm_w": ssm_norm_w, "ssm_w_out": ssm_w_out,
            "attn_w_qkv": attn_w_qkv, "attn_w_o": attn_w_o,
            "mlp_w1": mlp_w1, "mlp_w2": mlp_w2, "final_norm": final_norm}


def reference(x, norm_mix, norm_mlp, ssm_w_in, ssm_conv_w, ssm_conv_b, ssm_dt_bias,
              ssm_a_log, ssm_d, ssm_norm_w, ssm_w_out, attn_w_qkv, attn_w_o,
              mlp_w1, mlp_w2, final_norm):
    for i in range(DEPTH):
        h = rms_norm(x, norm_mix[i])
        j = i // N_MIXERS
        if i % N_MIXERS == 0:
            mix = mamba2_mixer(h, ssm_w_in[j], ssm_conv_w[j], ssm_conv_b[j], ssm_dt_bias[j],
                               ssm_a_log[j], ssm_d[j], ssm_norm_w[j], ssm_w_out[j])
        else:
            mix = dilated_attention(h, attn_w_qkv[j], attn_w_o[j])
        x = x + mix
        x = x + sq_relu_mlp(rms_norm(x, norm_mlp[i]), mlp_w1[i], mlp_w2[i])
    return rms_norm(x, final_norm)
```

```python
import functools
import math

import jax
import jax.numpy as jnp
from jax import lax
from jax.experimental import pallas as pl
from jax.experimental.pallas import tpu as pltpu

F32 = jnp.float32
BF16 = jnp.bfloat16

EPS = 1e-5
SSM_HEAD_DIM = 64
SSM_GROUPS = 8
SSM_STATE = 128
CONV_WIDTH = 4
CHUNK = 128
ATTN_GROUPS = ((128, 1), (512, 4), (2048, 16))
HEADS_PER_GROUP = 16
ATTN_HEAD_DIM = 64
ATTN_BLOCK = 128
LANES = 128
CARRY_ROWS = 8

VMEM_LIMIT = 56 * 1024 * 1024


def _params(*sem):
    return pltpu.CompilerParams(dimension_semantics=sem, vmem_limit_bytes=VMEM_LIMIT)


def _rms(x, g):
    return x * lax.rsqrt(jnp.mean(x * x, axis=-1, keepdims=True) + EPS) * g


def _silu(x):
    return x * (1.0 / (1.0 + jnp.exp(-x)))


def _softplus(v):
    return jnp.maximum(v, 0.0) + jnp.log1p(jnp.exp(-jnp.abs(v)))


def _split3(v):
    hi = v.astype(BF16)
    r = v - hi.astype(F32)
    mid = r.astype(BF16)
    lo = (r - mid.astype(F32)).astype(BF16)
    return hi, mid, lo


def _dot(a, b):
    return jnp.dot(a, b, preferred_element_type=F32)


def _dot_nt(a, b):
    return lax.dot_general(a, b, (((1,), (1,)), ((), ())), preferred_element_type=F32)


def _dot_tn(a, b):
    return lax.dot_general(a, b, (((0,), (0,)), ((), ())), preferred_element_type=F32)


def _expand(v, e):
    hi, mid, lo = _split3(v)
    return _dot(hi, e) + _dot(mid, e) + _dot(lo, e)


def _norm_matmul_kernel(x_ref, g_ref, w_ref, o_ref, h_ref):
    @pl.when(pl.program_id(1) == 0)
    def _():
        h_ref[...] = _rms(x_ref[...], g_ref[...]).astype(BF16)

    o_ref[...] = _dot(h_ref[...], w_ref[...]).astype(o_ref.dtype)


def _norm_matmul(x, g, w, out_dtype, tm, tn):
    m, k = x.shape
    n = w.shape[1]
    return pl.pallas_call(
        _norm_matmul_kernel,
        grid=(m // tm, n // tn),
        in_specs=[pl.BlockSpec((tm, k), lambda i, j: (i, 0)),
                  pl.BlockSpec((1, k), lambda i, j: (0, 0)),
                  pl.BlockSpec((k, tn), lambda i, j: (0, j))],
        out_specs=pl.BlockSpec((tm, tn), lambda i, j: (i, j)),
        out_shape=jax.ShapeDtypeStruct((m, n), out_dtype),
        scratch_shapes=[pltpu.VMEM((tm, k), BF16)],
        compiler_params=_params("parallel", "arbitrary"),
        name="norm_matmul",
    )(x, g, w)


def _inproj_kernel(x_ref, g_ref, w_ref, wdt_ref, dtb_ref, zx_ref, dt_ref, h_ref):
    @pl.when(pl.program_id(1) == 0)
    def _():
        h = _rms(x_ref[...], g_ref[...]).astype(BF16)
        h_ref[...] = h
        dt_ref[...] = _softplus(_dot(h, wdt_ref[...]) + dtb_ref[...])

    zx_ref[...] = _dot(h_ref[...], w_ref[...])


def _inproj(x, g, w_zx, w_dt, dt_bias, tm, tn):
    m, k = x.shape
    n = w_zx.shape[1]
    return pl.pallas_call(
        _inproj_kernel,
        grid=(m // tm, n // tn),
        in_specs=[pl.BlockSpec((tm, k), lambda i, j: (i, 0)),
                  pl.BlockSpec((1, k), lambda i, j: (0, 0)),
                  pl.BlockSpec((k, tn), lambda i, j: (0, j)),
                  pl.BlockSpec((k, LANES), lambda i, j: (0, 0)),
                  pl.BlockSpec((1, LANES), lambda i, j: (0, 0))],
        out_specs=[pl.BlockSpec((tm, tn), lambda i, j: (i, j)),
                   pl.BlockSpec((tm, LANES), lambda i, j: (i, 0))],
        out_shape=[jax.ShapeDtypeStruct((m, n), F32),
                   jax.ShapeDtypeStruct((m, LANES), F32)],
        scratch_shapes=[pltpu.VMEM((tm, k), BF16)],
        compiler_params=_params("parallel", "arbitrary"),
        name="ssm_in_proj",
    )(x, g, w_zx, w_dt, dt_bias)


def _matmul_res_kernel(a_ref, w_ref, r_ref, o_ref):
    o_ref[...] = r_ref[...] + _dot(a_ref[...], w_ref[...])


def _matmul_res(a, w, res, tm, tn):
    m, k = a.shape
    n = w.shape[1]
    return pl.pallas_call(
        _matmul_res_kernel,
        grid=(m // tm, n // tn),
        in_specs=[pl.BlockSpec((tm, k), lambda i, j: (i, 0)),
                  pl.BlockSpec((k, tn), lambda i, j: (0, j)),
                  pl.BlockSpec((tm, tn), lambda i, j: (i, j))],
        out_specs=pl.BlockSpec((tm, tn), lambda i, j: (i, j)),
        out_shape=jax.ShapeDtypeStruct((m, n), F32),
        compiler_params=_params("parallel", "arbitrary"),
        name="matmul_residual",
    )(a, w, res)


def _mlp_kernel(x_ref, g_ref, w1_ref, w2_ref, fg_ref, o_ref, h_ref, acc_ref, *, final_norm):
    f = pl.program_id(1)

    @pl.when(f == 0)
    def _():
        h_ref[...] = _rms(x_ref[...], g_ref[...]).astype(BF16)
        acc_ref[...] = jnp.zeros_like(acc_ref)

    a = jnp.maximum(_dot(h_ref[...], w1_ref[...]), 0.0)
    acc_ref[...] += _dot((a * a).astype(BF16), w2_ref[...])

    @pl.when(f == pl.num_programs(1) - 1)
    def _():
        y = x_ref[...] + acc_ref[...]
        if final_norm:
            y = _rms(y, fg_ref[...])
        o_ref[...] = y


def _mlp(x, g, w1, w2, fg, final_norm, tm, tf):
    m, d = x.shape
    ff = w1.shape[1]
    return pl.pallas_call(
        functools.partial(_mlp_kernel, final_norm=final_norm),
        grid=(m // tm, ff // tf),
        in_specs=[pl.BlockSpec((tm, d), lambda i, f: (i, 0)),
                  pl.BlockSpec((1, d), lambda i, f: (0, 0)),
                  pl.BlockSpec((d, tf), lambda i, f: (0, f)),
                  pl.BlockSpec((tf, d), lambda i, f: (f, 0)),
                  pl.BlockSpec((1, d), lambda i, f: (0, 0))],
        out_specs=pl.BlockSpec((tm, d), lambda i, f: (i, 0)),
        out_shape=jax.ShapeDtypeStruct((m, d), F32),
        scratch_shapes=[pltpu.VMEM((tm, d), BF16), pltpu.VMEM((tm, d), F32)],
        compiler_params=_params("parallel", "arbitrary"),
        name="mlp",
    )(x, g, w1, w2, fg)


def _ssd_kernel(zx_ref, dt_ref, cw_ref, cb_ref, alog_ref, dsk_ref, nw_ref, e_ref, y_ref,
                ubuf_ref, xc_ref, state_ref, *, d_inner):
    q = CHUNK
    conv_dim = d_inner + 2 * SSM_GROUPS * SSM_STATE
    gw = d_inner // SSM_GROUPS
    hpg = gw // SSM_HEAD_DIM

    @pl.when(pl.program_id(1) == 0)
    def _():
        ubuf_ref[0:CARRY_ROWS, :] = jnp.zeros((CARRY_ROWS, conv_dim), F32)
        state_ref[...] = jnp.zeros_like(state_ref)

    cpiece = 512
    for c0 in range(0, conv_dim, cpiece):
        cs = slice(c0, c0 + cpiece)
        ubuf_ref[CARRY_ROWS:CARRY_ROWS + q, cs] = zx_ref[0, :, d_inner + c0:d_inner + c0 + cpiece]
        acc = jnp.broadcast_to(cb_ref[:, cs], (q, cpiece))
        for k in range(CONV_WIDTH):
            r0 = CARRY_ROWS - (CONV_WIDTH - 1) + k
            acc = acc + cw_ref[k:k + 1, cs] * ubuf_ref[r0:r0 + q, cs]
        xc_ref[:, cs] = _silu(acc)
        ubuf_ref[0:CARRY_ROWS, cs] = ubuf_ref[q:q + CARRY_ROWS, cs]

    dt = dt_ref[0]
    a = dt * (-jnp.exp(alog_ref[...]))
    row = lax.broadcasted_iota(jnp.int32, (q, q), 0)
    col = lax.broadcasted_iota(jnp.int32, (q, q), 1)
    causal = row >= col
    tril = causal.astype(BF16)
    a_hi, a_mid, a_lo = _split3(a)
    a_cs = _dot(tril, a_hi) + _dot(tril, a_mid) + _dot(tril, a_lo)
    a_cs_t = a_cs.T
    a_last = a_cs[q - 1:q, :]

    e = e_ref[...]
    dt_x = _expand(dt, e)
    grow_x = _expand(jnp.exp(a_cs), e)
    dec_x = _expand(jnp.exp(a_last - a_cs), e)

    lane = lax.broadcasted_iota(jnp.int32, (q, LANES), 1)
    lo_half = lane < SSM_HEAD_DIM

    for g in range(SSM_GROUPS):
        gs = slice(g * gw, (g + 1) * gw)
        bg = xc_ref[:, d_inner + g * SSM_STATE:d_inner + (g + 1) * SSM_STATE].astype(BF16)
        cg = xc_ref[:, d_inner + (SSM_GROUPS + g) * SSM_STATE:
                    d_inner + (SSM_GROUPS + g + 1) * SSM_STATE].astype(BF16)
        cb = _dot_nt(cg, bg)
        xs = xc_ref[:, gs]
        xdt = xs * dt_x[:, gs]
        st = state_ref[:, gs]
        y = _dot(cg, st.astype(BF16)) * grow_x[:, gs]
        new_st = _dot_tn(bg, (xdt * dec_x[:, gs]).astype(BF16))
        state_ref[:, gs] = st * grow_x[q - 1:q, gs] + new_st

        ydiag = []
        for pr in range(hpg // 2):
            lhs = []
            for half in range(2):
                h = g * hpg + 2 * pr + half
                diff = a_cs[:, h:h + 1] - a_cs_t[h:h + 1, :]
                decay = jnp.exp(jnp.where(causal, diff, -jnp.inf))
                lhs.append((cb * decay).astype(BF16))
            xp = xdt[:, pr * LANES:(pr + 1) * LANES]
            rhs = jnp.concatenate([jnp.where(lo_half, xp, 0.0), jnp.where(lo_half, 0.0, xp)],
                                  axis=0).astype(BF16)
            ydiag.append(_dot(jnp.concatenate(lhs, axis=1), rhs))
        y = y + jnp.concatenate(ydiag, axis=1) + dsk_ref[:, gs] * xs

        gated = y * _silu(zx_ref[0, :, gs])
        yn = gated * lax.rsqrt(jnp.mean(gated * gated, axis=-1, keepdims=True) + EPS)
        y_ref[0, :, gs] = (yn * nw_ref[:, gs]).astype(y_ref.dtype)


def _ssd(zx, dt, conv_w, conv_b, a_log, d_skip_x, norm_w, expand, d_inner):
    bsz, s, n_zx = zx.shape
    conv_dim = n_zx - d_inner
    kern = functools.partial(_ssd_kernel, d_inner=d_inner)
    const = lambda b, c: (0, 0)
    return pl.pallas_call(
        kern,
        grid=(bsz, s // CHUNK),
        in_specs=[pl.BlockSpec((1, CHUNK, n_zx), lambda b, c: (b, c, 0)),
                  pl.BlockSpec((1, CHUNK, LANES), lambda b, c: (b, c, 0)),
                  pl.BlockSpec((CONV_WIDTH, conv_dim), const),
                  pl.BlockSpec((1, conv_dim), const),
                  pl.BlockSpec((1, LANES), const),
                  pl.BlockSpec((1, d_inner), const),
                  pl.BlockSpec((1, d_inner), const),
                  pl.BlockSpec((LANES, d_inner), const)],
        out_specs=pl.BlockSpec((1, CHUNK, d_inner), lambda b, c: (b, c, 0)),
        out_shape=jax.ShapeDtypeStruct((bsz, s, d_inner), BF16),
        scratch_shapes=[pltpu.VMEM((CARRY_ROWS + CHUNK, conv_dim), F32),
                        pltpu.VMEM((CHUNK, conv_dim), F32),
                        pltpu.VMEM((SSM_STATE, d_inner), F32)],
        compiler_params=_params("parallel", "arbitrary"),
        name="ssd_chunk",
    )(zx, dt, conv_w, conv_b, a_log, d_skip_x, norm_w, expand)


def _attn_kernel(slopes_ref, q_ref, kp_ref, kc_ref, vp_ref, vc_ref, o_ref, lse_ref, *, dilation):
    blk = ATTN_BLOCK
    n = pl.program_id(2)
    qi = lax.broadcasted_iota(jnp.int32, (blk, 2 * blk), 0)
    kj = lax.broadcasted_iota(jnp.int32, (blk, 2 * blk), 1)
    dist = qi + blk - kj
    valid = (dist >= 0) & (dist <= blk) & ((n > 0) | (kj >= blk))
    neg_dist = jnp.where(valid, -(dist * dilation).astype(F32), -jnp.inf)
    lo_half = lax.broadcasted_iota(jnp.int32, (blk, LANES), 1) < ATTN_HEAD_DIM
    lo_half2 = lax.broadcasted_iota(jnp.int32, (2 * blk, LANES), 1) < ATTN_HEAD_DIM
    scale = 1.0 / math.sqrt(ATTN_HEAD_DIM)

    for pr in range(HEADS_PER_GROUP // 2):
        ls = slice(pr * LANES, (pr + 1) * LANES)
        qp = q_ref[0, :, ls]
        kk = jnp.concatenate([kp_ref[0, :, ls], kc_ref[0, :, ls]], axis=0)
        vv = jnp.concatenate([vp_ref[0, :, ls], vc_ref[0, :, ls]], axis=0)
        ps, ms, dens = [], [], []
        for half in range(2):
            keep = lo_half if half == 0 else jnp.logical_not(lo_half)
            qh = jnp.where(keep, qp, jnp.zeros_like(qp))
            s = _dot_nt(qh, kk) * scale + slopes_ref[2 * pr + half] * neg_dist
            m = jnp.max(s, axis=-1, keepdims=True)
            p = jnp.exp(s - m)
            ps.append(p.astype(BF16))
            ms.append(m)
            dens.append(jnp.sum(p, axis=-1, keepdims=True))
        zero = jnp.zeros_like(vv)
        vcat = jnp.concatenate([jnp.where(lo_half2, vv, zero), jnp.where(lo_half2, zero, vv)], axis=0)
        o = _dot(jnp.concatenate(ps, axis=1), vcat)
        o_ref[0, :, ls] = o * jnp.where(lo_half, 1.0 / dens[0], 1.0 / dens[1])
        lse_ref[0, :, ls] = jnp.where(lo_half, ms[0] + jnp.log(dens[0]), ms[1] + jnp.log(dens[1]))


def _attn_group(qkv, slopes, g, dilation, n_groups):
    bsz, s, qkv_dim = qkv.shape
    w = HEADS_PER_GROUP * ATTN_HEAD_DIM
    lsub = s // dilation
    nb = lsub // ATTN_BLOCK
    cpr = qkv_dim // w
    view = qkv.reshape(bsz, lsub, dilation * qkv_dim)
    blk = (1, ATTN_BLOCK, w)

    def spec(which, prev):
        off = which * n_groups + g
        if prev:
            return pl.BlockSpec(blk, lambda b, r, n: (b, jnp.maximum(n - 1, 0), r * cpr + off))
        return pl.BlockSpec(blk, lambda b, r, n: (b, n, r * cpr + off))

    out_spec = pl.BlockSpec(blk, lambda b, r, n: (b, n, r))
    o, lse = pl.pallas_call(
        functools.partial(_attn_kernel, dilation=dilation),
        grid=(bsz, dilation, nb),
        in_specs=[pl.BlockSpec(memory_space=pltpu.SMEM),
                  spec(0, False), spec(1, True), spec(1, False), spec(2, True), spec(2, False)],
        out_specs=[out_spec, out_spec],
        out_shape=[jax.ShapeDtypeStruct((bsz, lsub, dilation * w), F32)] * 2,
        compiler_params=_params("parallel", "parallel", "arbitrary"),
        name=f"dilated_attn_g{g}",
    )(slopes, view, view, view, view, view)
    return o.reshape(bsz, s, w), lse.reshape(bsz, s, w)


def _attn_out_kernel(o0_ref, o1_ref, o2_ref, l0_ref, l1_ref, l2_ref, w_ref, x_ref, out_ref):
    l0, l1, l2 = l0_ref[...], l1_ref[...], l2_ref[...]
    m = jnp.maximum(jnp.maximum(l0, l1), l2)
    e0, e1, e2 = jnp.exp(l0 - m), jnp.exp(l1 - m), jnp.exp(l2 - m)
    mix = (e0 * o0_ref[...] + e1 * o1_ref[...] + e2 * o2_ref[...]) / (e0 + e1 + e2)
    out_ref[...] = x_ref[...] + _dot(mix.astype(BF16), w_ref[...])


def _attn_out(os_, lses, w_o, x, tm):
    m, w = os_[0].shape
    d = w_o.shape[1]
    act = pl.BlockSpec((tm, w), lambda i: (i, 0))
    return pl.pallas_call(
        _attn_out_kernel,
        grid=(m // tm,),
        in_specs=[act] * 6 + [pl.BlockSpec((w, d), lambda i: (0, 0)),
                              pl.BlockSpec((tm, d), lambda i: (i, 0))],
        out_specs=pl.BlockSpec((tm, d), lambda i: (i, 0)),
        out_shape=jax.ShapeDtypeStruct((m, d), F32),
        compiler_params=_params("parallel"),
        name="attn_combine_out_proj",
    )(*os_, *lses, w_o, x)


def _alibi_slopes(n_groups):
    n = n_groups * HEADS_PER_GROUP
    i = jnp.arange(1, n + 1, dtype=F32)
    return jnp.exp2(-8.0 * i / n).reshape(n_groups, HEADS_PER_GROUP)


def _mamba_layer(x2, bsz, s, g_mix, w_in, conv_w, conv_b, dt_bias, a_log, d_skip, norm_w, w_out):
    d_model = x2.shape[1]
    d_inner = w_out.shape[0]
    heads = d_inner // SSM_HEAD_DIM
    n_zx = w_in.shape[1] - heads
    assert heads <= LANES and s % CHUNK == 0
    w_zx = w_in[:, :n_zx].astype(BF16)
    w_dt = jnp.pad(w_in[:, n_zx:], ((0, 0), (0, LANES - heads))).astype(BF16)
    dtb = jnp.pad(dt_bias, (0, LANES - heads)).reshape(1, LANES)
    alog = jnp.pad(a_log, (0, LANES - heads)).reshape(1, LANES)
    zx, dt = _inproj(x2, g_mix.reshape(1, d_model), w_zx, w_dt, dtb, tm=512, tn=1024)

    lane_head = jnp.arange(d_inner, dtype=jnp.int32) // SSM_HEAD_DIM
    expand = (jnp.arange(LANES, dtype=jnp.int32)[:, None] == lane_head[None, :]).astype(BF16)
    d_skip_x = jnp.repeat(d_skip, SSM_HEAD_DIM).reshape(1, d_inner)
    y = _ssd(zx.reshape(bsz, s, n_zx), dt.reshape(bsz, s, LANES), conv_w,
             conv_b.reshape(1, -1), alog, d_skip_x, norm_w.reshape(1, d_inner), expand, d_inner)
    return _matmul_res(y.reshape(bsz * s, d_inner), w_out.astype(BF16), x2, tm=512, tn=1024)


def _attn_layer(x2, bsz, s, g_mix, w_qkv, w_o):
    d_model = x2.shape[1]
    n_groups = len(ATTN_GROUPS)
    assert all(win // dil == ATTN_BLOCK and s % win == 0 for win, dil in ATTN_GROUPS)
    qkv = _norm_matmul(x2, g_mix.reshape(1, d_model), w_qkv.astype(BF16), BF16, tm=512, tn=1024)
    qkv = qkv.reshape(bsz, s, -1)
    slopes = _alibi_slopes(n_groups)
    os_, lses = [], []
    for g, (_, dilation) in enumerate(ATTN_GROUPS):
        o, lse = _attn_group(qkv, slopes[g], g, dilation, n_groups)
        os_.append(o.reshape(bsz * s, -1))
        lses.append(lse.reshape(bsz * s, -1))
    return _attn_out(os_, lses, w_o.astype(BF16), x2, tm=256)


def kernel(x, norm_mix, norm_mlp, ssm_w_in, ssm_conv_w, ssm_conv_b, ssm_dt_bias, ssm_a_log, ssm_d,
           ssm_norm_w, ssm_w_out, attn_w_qkv, attn_w_o, mlp_w1, mlp_w2, final_norm):
    bsz, s, d_model = x.shape
    depth = norm_mix.shape[0]
    x2 = x.reshape(bsz * s, d_model)
    fg = final_norm.reshape(1, d_model)
    for i in range(depth):
        j = i // 2
        if i % 2 == 0:
            x2 = _mamba_layer(x2, bsz, s, norm_mix[i], ssm_w_in[j], ssm_conv_w[j], ssm_conv_b[j],
                              ssm_dt_bias[j], ssm_a_log[j], ssm_d[j], ssm_norm_w[j], ssm_w_out[j])
        else:
            x2 = _attn_layer(x2, bsz, s, norm_mix[i], attn_w_qkv[j], attn_w_o[j])
        x2 = _mlp(x2, norm_mlp[i].reshape(1, d_model), mlp_w1[i].astype(BF16),
                  mlp_w2[i].astype(BF16), fg, final_norm=(i == depth - 1), tm=512, tf=512)
    return x2.reshape(bsz, s, d_model)
```

```python
import functools
import math

import jax
import jax.numpy as jnp
from jax import lax
from jax.experimental import pallas as pl
from jax.experimental.pallas import tpu as pltpu

F32 = jnp.float32
BF16 = jnp.bfloat16

EPS = 1e-5
SSM_HEAD_DIM = 64
SSM_GROUPS = 8
SSM_STATE = 128
CONV_WIDTH = 4
CHUNK = 128
ATTN_GROUPS = ((128, 1), (512, 4), (2048, 16))
HEADS_PER_GROUP = 16
ATTN_HEAD_DIM = 64
ATTN_BLOCK = 128
LANES = 128
CARRY_ROWS = 8

VMEM_LIMIT = 56 * 1024 * 1024


def _params(*sem):
    return pltpu.CompilerParams(dimension_semantics=sem, vmem_limit_bytes=VMEM_LIMIT)


def _rms(x, g):
    return x * lax.rsqrt(jnp.mean(x * x, axis=-1, keepdims=True) + EPS) * g


def _silu(x):
    return x * (1.0 / (1.0 + jnp.exp(-x)))


def _softplus(v):
    return jnp.maximum(v, 0.0) + jnp.log1p(jnp.exp(-jnp.abs(v)))


def _split3(v):
    hi = v.astype(BF16)
    r = v - hi.astype(F32)
    mid = r.astype(BF16)
    lo = (r - mid.astype(F32)).astype(BF16)
    return hi, mid, lo


def _dot(a, b):
    return jnp.dot(a, b, preferred_element_type=F32)


def _dot_nt(a, b):
    return lax.dot_general(a, b, (((1,), (1,)), ((), ())), preferred_element_type=F32)


def _dot_tn(a, b):
    return lax.dot_general(a, b, (((0,), (0,)), ((), ())), preferred_element_type=F32)


def _expand(v, e):
    hi, mid, lo = _split3(v)
    return _dot(hi, e) + _dot(mid, e) + _dot(lo, e)


def _perm_view_shape(bsz, s, dilation, tm, c):
    span = ATTN_BLOCK * dilation
    return (bsz, s // span, dilation, span // tm, tm // dilation, c)


def _perm_block_spec(s, dilation, tm, cblk, col_of):
    span = ATTN_BLOCK * dilation
    per_seq, per_span = s // tm, span // tm

    def index(i, *rest):
        ib = i % per_seq
        return (i // per_seq, ib // per_span, 0, ib % per_span, 0, col_of(*rest))

    return pl.BlockSpec((None, None, dilation, None, tm // dilation, cblk), index)


def _qkv_proj_kernel(x_ref, g_ref, w_ref, *refs, dilations):
    ng = len(dilations)
    o_refs, hf_ref, hp_ref = refs[:ng], refs[ng], refs[ng + 1]
    j = pl.program_id(1)
    tm = x_ref.shape[0]

    @pl.when(j == 0)
    def _():
        h = _rms(x_ref[...], g_ref[...])
        for c in range(hf_ref.shape[0]):
            cs = slice(c * LANES, (c + 1) * LANES)
            hf_ref[c] = h[:, cs]
            for gi, d in enumerate(dilations):
                if d == 1:
                    hp_ref[gi, :, cs] = h[:, cs].astype(BF16)
                    continue
                p = tm // d
                for r in range(d):
                    hp_ref[gi, r * p:(r + 1) * p, cs] = hf_ref[c, pl.ds(r, p, stride=d), :].astype(BF16)

    res = _dot(hp_ref[j // 3], w_ref[...]).astype(BF16)
    for gi, d in enumerate(dilations):
        @pl.when(j // 3 == gi)
        def _(gi=gi, d=d):
            if d == 1:
                o_refs[gi][...] = res
            else:
                p = tm // d
                for r in range(d):
                    o_refs[gi][r] = res[r * p:(r + 1) * p, :]


def _qkv_proj(x, g, w, bsz, s, dilations, tm, tn):
    m, k = x.shape
    ng = len(dilations)
    wq = w.shape[1] // ng
    assert wq == 3 * tn and s % tm == 0
    out_specs, out_shapes = [], []
    for gi, d in enumerate(dilations):
        col_of = lambda j, gi=gi: jnp.clip(j - 3 * gi, 0, 2)
        if d == 1:
            out_specs.append(pl.BlockSpec((tm, tn), lambda i, j, c=col_of: (i, c(j))))
            out_shapes.append(jax.ShapeDtypeStruct((m, wq), BF16))
        else:
            out_specs.append(_perm_block_spec(s, d, tm, tn, col_of))
            out_shapes.append(jax.ShapeDtypeStruct(_perm_view_shape(bsz, s, d, tm, wq), BF16))
    outs = pl.pallas_call(
        functools.partial(_qkv_proj_kernel, dilations=dilations),
        grid=(m // tm, 3 * ng),
        in_specs=[pl.BlockSpec((tm, k), lambda i, j: (i, 0)),
                  pl.BlockSpec((1, k), lambda i, j: (0, 0)),
                  pl.BlockSpec((k, tn), lambda i, j: (0, j))],
        out_specs=out_specs,
        out_shape=out_shapes,
        scratch_shapes=[pltpu.VMEM((k // LANES, tm, LANES), F32), pltpu.VMEM((ng, tm, k), BF16)],
        compiler_params=_params("parallel", "arbitrary"),
        name="qkv_proj",
    )(x, g, w)
    return [o.reshape(bsz, s, wq) for o in outs]


def _inproj_kernel(x_ref, g_ref, w_ref, wdt_ref, dtb_ref, zx_ref, dt_ref, h_ref):
    @pl.when(pl.program_id(1) == 0)
    def _():
        h = _rms(x_ref[...], g_ref[...]).astype(BF16)
        h_ref[...] = h
        dt_ref[...] = _softplus(_dot(h, wdt_ref[...]) + dtb_ref[...])

    zx_ref[...] = _dot(h_ref[...], w_ref[...])


def _inproj(x, g, w_zx, w_dt, dt_bias, tm, tn):
    m, k = x.shape
    n = w_zx.shape[1]
    return pl.pallas_call(
        _inproj_kernel,
        grid=(m // tm, n // tn),
        in_specs=[pl.BlockSpec((tm, k), lambda i, j: (i, 0)),
                  pl.BlockSpec((1, k), lambda i, j: (0, 0)),
                  pl.BlockSpec((k, tn), lambda i, j: (0, j)),
                  pl.BlockSpec((k, LANES), lambda i, j: (0, 0)),
                  pl.BlockSpec((1, LANES), lambda i, j: (0, 0))],
        out_specs=[pl.BlockSpec((tm, tn), lambda i, j: (i, j)),
                   pl.BlockSpec((tm, LANES), lambda i, j: (i, 0))],
        out_shape=[jax.ShapeDtypeStruct((m, n), F32),
                   jax.ShapeDtypeStruct((m, LANES), F32)],
        scratch_shapes=[pltpu.VMEM((tm, k), BF16)],
        compiler_params=_params("parallel", "arbitrary"),
        name="ssm_in_proj",
    )(x, g, w_zx, w_dt, dt_bias)


def _matmul_res_kernel(a_ref, w_ref, r_ref, o_ref):
    o_ref[...] = r_ref[...] + _dot(a_ref[...], w_ref[...])


def _matmul_res(a, w, res, tm, tn):
    m, k = a.shape
    n = w.shape[1]
    return pl.pallas_call(
        _matmul_res_kernel,
        grid=(m // tm, n // tn),
        in_specs=[pl.BlockSpec((tm, k), lambda i, j: (i, 0)),
                  pl.BlockSpec((k, tn), lambda i, j: (0, j)),
                  pl.BlockSpec((tm, tn), lambda i, j: (i, j))],
        out_specs=pl.BlockSpec((tm, tn), lambda i, j: (i, j)),
        out_shape=jax.ShapeDtypeStruct((m, n), F32),
        compiler_params=_params("parallel", "arbitrary"),
        name="matmul_residual",
    )(a, w, res)


def _mlp_kernel(x_ref, g_ref, w1_ref, w2_ref, fg_ref, o_ref, h_ref, acc_ref, *, final_norm):
    f = pl.program_id(1)

    @pl.when(f == 0)
    def _():
        h_ref[...] = _rms(x_ref[...], g_ref[...]).astype(BF16)
        acc_ref[...] = jnp.zeros_like(acc_ref)

    a = jnp.maximum(_dot(h_ref[...], w1_ref[...]), 0.0)
    acc_ref[...] += _dot((a * a).astype(BF16), w2_ref[...])

    @pl.when(f == pl.num_programs(1) - 1)
    def _():
        y = x_ref[...] + acc_ref[...]
        if final_norm:
            y = _rms(y, fg_ref[...])
        o_ref[...] = y


def _mlp(x, g, w1, w2, fg, final_norm, tm, tf):
    m, d = x.shape
    ff = w1.shape[1]
    return pl.pallas_call(
        functools.partial(_mlp_kernel, final_norm=final_norm),
        grid=(m // tm, ff // tf),
        in_specs=[pl.BlockSpec((tm, d), lambda i, f: (i, 0)),
                  pl.BlockSpec((1, d), lambda i, f: (0, 0)),
                  pl.BlockSpec((d, tf), lambda i, f: (0, f)),
                  pl.BlockSpec((tf, d), lambda i, f: (f, 0)),
                  pl.BlockSpec((1, d), lambda i, f: (0, 0))],
        out_specs=pl.BlockSpec((tm, d), lambda i, f: (i, 0)),
        out_shape=jax.ShapeDtypeStruct((m, d), F32),
        scratch_shapes=[pltpu.VMEM((tm, d), BF16), pltpu.VMEM((tm, d), F32)],
        compiler_params=_params("parallel", "arbitrary"),
        name="mlp",
    )(x, g, w1, w2, fg)


def _ssd_kernel(zx_ref, dt_ref, cw_ref, cb_ref, alog_ref, dsk_ref, nw_ref, e_ref, y_ref,
                ubuf_ref, xc_ref, state_ref, *, d_inner):
    q = CHUNK
    conv_dim = d_inner + 2 * SSM_GROUPS * SSM_STATE
    gw = d_inner // SSM_GROUPS
    hpg = gw // SSM_HEAD_DIM

    @pl.when(pl.program_id(1) == 0)
    def _():
        ubuf_ref[0:CARRY_ROWS, :] = jnp.zeros((CARRY_ROWS, conv_dim), F32)
        state_ref[...] = jnp.zeros_like(state_ref)

    cpiece = 512
    for c0 in range(0, conv_dim, cpiece):
        cs = slice(c0, c0 + cpiece)
        ubuf_ref[CARRY_ROWS:CARRY_ROWS + q, cs] = zx_ref[0, :, d_inner + c0:d_inner + c0 + cpiece]
        acc = jnp.broadcast_to(cb_ref[:, cs], (q, cpiece))
        for k in range(CONV_WIDTH):
            r0 = CARRY_ROWS - (CONV_WIDTH - 1) + k
            acc = acc + cw_ref[k:k + 1, cs] * ubuf_ref[r0:r0 + q, cs]
        xc_ref[:, cs] = _silu(acc)
        ubuf_ref[0:CARRY_ROWS, cs] = ubuf_ref[q:q + CARRY_ROWS, cs]

    dt = dt_ref[0]
    a = dt * (-jnp.exp(alog_ref[...]))
    row = lax.broadcasted_iota(jnp.int32, (q, q), 0)
    col = lax.broadcasted_iota(jnp.int32, (q, q), 1)
    causal = row >= col
    tril = causal.astype(BF16)
    a_hi, a_mid, a_lo = _split3(a)
    a_cs = _dot(tril, a_hi) + _dot(tril, a_mid) + _dot(tril, a_lo)
    a_cs_t = a_cs.T
    a_last = a_cs[q - 1:q, :]

    e = e_ref[...]
    dt_x = _expand(dt, e)
    grow_x = _expand(jnp.exp(a_cs), e)
    dec_x = _expand(jnp.exp(a_last - a_cs), e)

    lane = lax.broadcasted_iota(jnp.int32, (q, LANES), 1)
    lo_half = lane < SSM_HEAD_DIM

    for g in range(SSM_GROUPS):
        gs = slice(g * gw, (g + 1) * gw)
        bg = xc_ref[:, d_inner + g * SSM_STATE:d_inner + (g + 1) * SSM_STATE].astype(BF16)
        cg = xc_ref[:, d_inner + (SSM_GROUPS + g) * SSM_STATE:
                    d_inner + (SSM_GROUPS + g + 1) * SSM_STATE].astype(BF16)
        cb = _dot_nt(cg, bg)
        xs = xc_ref[:, gs]
        xdt = xs * dt_x[:, gs]
        st = state_ref[:, gs]
        y = _dot(cg, st.astype(BF16)) * grow_x[:, gs]
        new_st = _dot_tn(bg, (xdt * dec_x[:, gs]).astype(BF16))
        state_ref[:, gs] = st * grow_x[q - 1:q, gs] + new_st

        ydiag = []
        for pr in range(hpg // 2):
            lhs = []
            for half in range(2):
                h = g * hpg + 2 * pr + half
                diff = a_cs[:, h:h + 1] - a_cs_t[h:h + 1, :]
                decay = jnp.exp(jnp.where(causal, diff, -jnp.inf))
                lhs.append((cb * decay).astype(BF16))
            xp = xdt[:, pr * LANES:(pr + 1) * LANES]
            rhs = jnp.concatenate([jnp.where(lo_half, xp, 0.0), jnp.where(lo_half, 0.0, xp)],
                                  axis=0).astype(BF16)
            ydiag.append(_dot(jnp.concatenate(lhs, axis=1), rhs))
        y = y + jnp.concatenate(ydiag, axis=1) + dsk_ref[:, gs] * xs

        gated = y * _silu(zx_ref[0, :, gs])
        yn = gated * lax.rsqrt(jnp.mean(gated * gated, axis=-1, keepdims=True) + EPS)
        y_ref[0, :, gs] = (yn * nw_ref[:, gs]).astype(y_ref.dtype)


def _ssd(zx, dt, conv_w, conv_b, a_log, d_skip_x, norm_w, expand, d_inner):
    bsz, s, n_zx = zx.shape
    conv_dim = n_zx - d_inner
    kern = functools.partial(_ssd_kernel, d_inner=d_inner)
    const = lambda b, c: (0, 0)
    return pl.pallas_call(
        kern,
        grid=(bsz, s // CHUNK),
        in_specs=[pl.BlockSpec((1, CHUNK, n_zx), lambda b, c: (b, c, 0)),
                  pl.BlockSpec((1, CHUNK, LANES), lambda b, c: (b, c, 0)),
                  pl.BlockSpec((CONV_WIDTH, conv_dim), const),
                  pl.BlockSpec((1, conv_dim), const),
                  pl.BlockSpec((1, LANES), const),
                  pl.BlockSpec((1, d_inner), const),
                  pl.BlockSpec((1, d_inner), const),
                  pl.BlockSpec((LANES, d_inner), const)],
        out_specs=pl.BlockSpec((1, CHUNK, d_inner), lambda b, c: (b, c, 0)),
        out_shape=jax.ShapeDtypeStruct((bsz, s, d_inner), BF16),
        scratch_shapes=[pltpu.VMEM((CARRY_ROWS + CHUNK, conv_dim), F32),
                        pltpu.VMEM((CHUNK, conv_dim), F32),
                        pltpu.VMEM((SSM_STATE, d_inner), F32)],
        compiler_params=_params("parallel", "arbitrary"),
        name="ssd_chunk",
    )(zx, dt, conv_w, conv_b, a_log, d_skip_x, norm_w, expand)


def _attn_kernel(slopes_ref, q_ref, kp_ref, kc_ref, vp_ref, vc_ref, o_ref, lse_ref, *, dilation):
    blk = ATTN_BLOCK
    has_prev = pl.program_id(1) >= dilation
    qi = lax.broadcasted_iota(jnp.int32, (blk, 2 * blk), 0)
    kj = lax.broadcasted_iota(jnp.int32, (blk, 2 * blk), 1)
    dist = qi + blk - kj
    valid = (dist >= 0) & (dist <= blk) & (has_prev | (kj >= blk))
    neg_dist = jnp.where(valid, -(dist * dilation).astype(F32), -jnp.inf)
    lo_half = lax.broadcasted_iota(jnp.int32, (blk, LANES), 1) < ATTN_HEAD_DIM
    lo_half2 = lax.broadcasted_iota(jnp.int32, (2 * blk, LANES), 1) < ATTN_HEAD_DIM
    scale = 1.0 / math.sqrt(ATTN_HEAD_DIM)

    for pr in range(HEADS_PER_GROUP // 2):
        ls = slice(pr * LANES, (pr + 1) * LANES)
        qp = q_ref[0, :, ls]
        kk = jnp.concatenate([kp_ref[0, :, ls], kc_ref[0, :, ls]], axis=0)
        vv = jnp.concatenate([vp_ref[0, :, ls], vc_ref[0, :, ls]], axis=0)
        ps, ms, dens = [], [], []
        for half in range(2):
            keep = lo_half if half == 0 else jnp.logical_not(lo_half)
            qh = jnp.where(keep, qp, jnp.zeros_like(qp))
            s = _dot_nt(qh, kk) * scale + slopes_ref[2 * pr + half] * neg_dist
            m = jnp.max(s, axis=-1, keepdims=True)
            p = jnp.exp(s - m)
            ps.append(p.astype(BF16))
            ms.append(m)
            dens.append(jnp.sum(p, axis=-1, keepdims=True))
        zero = jnp.zeros_like(vv)
        vcat = jnp.concatenate([jnp.where(lo_half2, vv, zero), jnp.where(lo_half2, zero, vv)], axis=0)
        o = _dot(jnp.concatenate(ps, axis=1), vcat)
        o_ref[0, :, ls] = o * jnp.where(lo_half, 1.0 / dens[0], 1.0 / dens[1])
        lse_ref[0, :, ls] = jnp.where(lo_half, ms[0] + jnp.log(dens[0]), ms[1] + jnp.log(dens[1]))


def _attn_group(qkv, slopes, g, dilation):
    bsz, s, _ = qkv.shape
    w = HEADS_PER_GROUP * ATTN_HEAD_DIM
    blk = (1, ATTN_BLOCK, w)

    def spec(which, prev):
        if prev:
            return pl.BlockSpec(blk, lambda b, n: (b, jnp.maximum(n - dilation, 0), which))
        return pl.BlockSpec(blk, lambda b, n: (b, n, which))

    out_spec = pl.BlockSpec(blk, lambda b, n: (b, n, 0))
    return pl.pallas_call(
        functools.partial(_attn_kernel, dilation=dilation),
        grid=(bsz, s // ATTN_BLOCK),
        in_specs=[pl.BlockSpec(memory_space=pltpu.SMEM),
                  spec(0, False), spec(1, True), spec(1, False), spec(2, True), spec(2, False)],
        out_specs=[out_spec, out_spec],
        out_shape=[jax.ShapeDtypeStruct((bsz, s, w), F32)] * 2,
        compiler_params=_params("parallel", "arbitrary"),
        name=f"dilated_attn_g{g}",
    )(slopes, qkv, qkv, qkv, qkv, qkv)


def _attn_out_kernel(*refs, dilations):
    ng = len(dilations)
    o_refs, l_refs = refs[:ng], refs[ng:2 * ng]
    w_ref, x_ref, out_ref, mix_ref = refs[2 * ng:2 * ng + 4]
    nat_refs = refs[2 * ng + 4:]
    tm = x_ref.shape[0]
    for c in range(mix_ref.shape[1] // LANES):
        cs = slice(c * LANES, (c + 1) * LANES)
        os_, ls_ = [], []
        k = 0
        for gi, d in enumerate(dilations):
            if d == 1:
                os_.append(o_refs[gi][:, cs])
                ls_.append(l_refs[gi][:, cs])
                continue
            p = tm // d
            for src, dst in ((o_refs[gi], nat_refs[k]), (l_refs[gi], nat_refs[k + 1])):
                for r in range(d):
                    dst[c, pl.ds(r, p, stride=d), :] = src[r, :, cs]
            os_.append(nat_refs[k][c])
            ls_.append(nat_refs[k + 1][c])
            k += 2
        m = functools.reduce(jnp.maximum, ls_)
        es = [jnp.exp(l - m) for l in ls_]
        mix = sum(e * o for e, o in zip(es, os_)) / sum(es)
        mix_ref[:, cs] = mix.astype(BF16)
    out_ref[...] = x_ref[...] + _dot(mix_ref[...], w_ref[...])


def _attn_out(os_, lses, w_o, x, bsz, s, dilations, tm):
    m, d_model = x.shape
    w = w_o.shape[0]
    specs, views = [], []
    for arrs in (os_, lses):
        for arr, d in zip(arrs, dilations):
            if d == 1:
                specs.append(pl.BlockSpec((tm, w), lambda i: (i, 0)))
                views.append(arr.reshape(m, w))
            else:
                specs.append(_perm_block_spec(s, d, tm, w, lambda: 0))
                views.append(arr.reshape(_perm_view_shape(bsz, s, d, tm, w)))
    n_perm = sum(d != 1 for d in dilations)
    return pl.pallas_call(
        functools.partial(_attn_out_kernel, dilations=dilations),
        grid=(m // tm,),
        in_specs=specs + [pl.BlockSpec((w, d_model), lambda i: (0, 0)),
                          pl.BlockSpec((tm, d_model), lambda i: (i, 0))],
        out_specs=pl.BlockSpec((tm, d_model), lambda i: (i, 0)),
        out_shape=jax.ShapeDtypeStruct((m, d_model), F32),
        scratch_shapes=[pltpu.VMEM((tm, w), BF16)] + [pltpu.VMEM((w // LANES, tm, LANES), F32)] * (2 * n_perm),
        compiler_params=_params("parallel"),
        name="attn_combine_out_proj",
    )(*views, w_o, x)


def _alibi_slopes(n_groups):
    n = n_groups * HEADS_PER_GROUP
    i = jnp.arange(1, n + 1, dtype=F32)
    return jnp.exp2(-8.0 * i / n).reshape(n_groups, HEADS_PER_GROUP)


def _mamba_layer(x2, bsz, s, g_mix, w_in, conv_w, conv_b, dt_bias, a_log, d_skip, norm_w, w_out):
    d_model = x2.shape[1]
    d_inner = w_out.shape[0]
    heads = d_inner // SSM_HEAD_DIM
    n_zx = w_in.shape[1] - heads
    assert heads <= LANES and s % CHUNK == 0
    w_zx = w_in[:, :n_zx].astype(BF16)
    w_dt = jnp.pad(w_in[:, n_zx:], ((0, 0), (0, LANES - heads))).astype(BF16)
    dtb = jnp.pad(dt_bias, (0, LANES - heads)).reshape(1, LANES)
    alog = jnp.pad(a_log, (0, LANES - heads)).reshape(1, LANES)
    zx, dt = _inproj(x2, g_mix.reshape(1, d_model), w_zx, w_dt, dtb, tm=512, tn=1024)

    lane_head = jnp.arange(d_inner, dtype=jnp.int32) // SSM_HEAD_DIM
    expand = (jnp.arange(LANES, dtype=jnp.int32)[:, None] == lane_head[None, :]).astype(BF16)
    d_skip_x = jnp.repeat(d_skip, SSM_HEAD_DIM).reshape(1, d_inner)
    y = _ssd(zx.reshape(bsz, s, n_zx), dt.reshape(bsz, s, LANES), conv_w,
             conv_b.reshape(1, -1), alog, d_skip_x, norm_w.reshape(1, d_inner), expand, d_inner)
    return _matmul_res(y.reshape(bsz * s, d_inner), w_out.astype(BF16), x2, tm=512, tn=1024)


def _attn_layer(x2, bsz, s, g_mix, w_qkv, w_o):
    d_model = x2.shape[1]
    n_groups = len(ATTN_GROUPS)
    assert all(win // dil == ATTN_BLOCK and s % win == 0 for win, dil in ATTN_GROUPS)
    dilations = tuple(dil for _, dil in ATTN_GROUPS)
    w = HEADS_PER_GROUP * ATTN_HEAD_DIM
    w_g = w_qkv.reshape(d_model, 3, n_groups, w).transpose(0, 2, 1, 3).reshape(d_model, -1)
    qkvs = _qkv_proj(x2, g_mix.reshape(1, d_model), w_g.astype(BF16), bsz, s, dilations,
                     tm=512, tn=w)
    slopes = _alibi_slopes(n_groups)
    os_, lses = [], []
    for g, dilation in enumerate(dilations):
        o, lse = _attn_group(qkvs[g], slopes[g], g, dilation)
        os_.append(o)
        lses.append(lse)
    return _attn_out(os_, lses, w_o.astype(BF16), x2, bsz, s, dilations, tm=256)


def kernel(x, norm_mix, norm_mlp, ssm_w_in, ssm_conv_w, ssm_conv_b, ssm_dt_bias, ssm_a_log, ssm_d,
           ssm_norm_w, ssm_w_out, attn_w_qkv, attn_w_o, mlp_w1, mlp_w2, final_norm):
    bsz, s, d_model = x.shape
    depth = norm_mix.shape[0]
    x2 = x.reshape(bsz * s, d_model)
    fg = final_norm.reshape(1, d_model)
    for i in range(depth):
        j = i // 2
        if i % 2 == 0:
            x2 = _mamba_layer(x2, bsz, s, norm_mix[i], ssm_w_in[j], ssm_conv_w[j], ssm_conv_b[j],
                              ssm_dt_bias[j], ssm_a_log[j], ssm_d[j], ssm_norm_w[j], ssm_w_out[j])
        else:
            x2 = _attn_layer(x2, bsz, s, norm_mix[i], attn_w_qkv[j], attn_w_o[j])
        x2 = _mlp(x2, norm_mlp[i].reshape(1, d_model), mlp_w1[i].astype(BF16),
                  mlp_w2[i].astype(BF16), fg, final_norm=(i == depth - 1), tm=512, tf=512)
    return x2.reshape(bsz, s, d_model)
```

```python
import functools
import math

import jax
import jax.numpy as jnp
from jax import lax
from jax.experimental import pallas as pl
from jax.experimental.pallas import tpu as pltpu

F32 = jnp.float32
BF16 = jnp.bfloat16

EPS = 1e-5
SSM_HEAD_DIM = 64
SSM_GROUPS = 8
SSM_STATE = 128
CONV_WIDTH = 4
CHUNK = 128
ATTN_GROUPS = ((128, 1), (512, 4), (2048, 16))
HEADS_PER_GROUP = 16
ATTN_HEAD_DIM = 64
ATTN_BLOCK = 128
LANES = 128
CARRY_ROWS = 8

VMEM_LIMIT = 56 * 1024 * 1024


def _params(*sem):
    return pltpu.CompilerParams(dimension_semantics=sem, vmem_limit_bytes=VMEM_LIMIT)


def _rms(x, g):
    return x * lax.rsqrt(jnp.mean(x * x, axis=-1, keepdims=True) + EPS) * g


def _silu(x):
    h = 0.5 * x
    return h + h * jnp.tanh(h)


def _softplus(v):
    return jnp.maximum(v, 0.0) + jnp.log1p(jnp.exp(-jnp.abs(v)))


def _split3(v):
    hi = v.astype(BF16)
    r = v - hi.astype(F32)
    mid = r.astype(BF16)
    lo = (r - mid.astype(F32)).astype(BF16)
    return hi, mid, lo


def _dot(a, b):
    return jnp.dot(a, b, preferred_element_type=F32)


def _dot_nt(a, b):
    return lax.dot_general(a, b, (((1,), (1,)), ((), ())), preferred_element_type=F32)


def _dot_tn(a, b):
    return lax.dot_general(a, b, (((0,), (0,)), ((), ())), preferred_element_type=F32)


def _expand(v, e2):
    hi = v.astype(BF16)
    lo = (v - hi.astype(F32)).astype(BF16)
    return _dot(jnp.concatenate([hi, lo], axis=1), e2)


def _perm_view_shape(bsz, s, dilation, tm, c):
    span = ATTN_BLOCK * dilation
    return (bsz, s // span, dilation, span // tm, tm // dilation, c)


def _perm_block_spec(s, dilation, tm, cblk, col_of):
    span = ATTN_BLOCK * dilation
    per_seq, per_span = s // tm, span // tm

    def index(i, *rest):
        ib = i % per_seq
        return (i // per_seq, ib // per_span, 0, ib % per_span, 0, col_of(*rest))

    return pl.BlockSpec((None, None, dilation, None, tm // dilation, cblk), index)


def _qkv_proj_kernel(x_ref, g_ref, w_ref, *refs, dilations):
    ng = len(dilations)
    o_refs, hf_ref, hp_ref = refs[:ng], refs[ng], refs[ng + 1]
    j = pl.program_id(1)
    tm = x_ref.shape[0]

    @pl.when(j == 0)
    def _():
        h = _rms(x_ref[...], g_ref[...])
        for c in range(hf_ref.shape[0]):
            cs = slice(c * LANES, (c + 1) * LANES)
            hf_ref[c] = h[:, cs]
            for gi, d in enumerate(dilations):
                if d == 1:
                    hp_ref[gi, :, cs] = h[:, cs].astype(BF16)
                    continue
                p = tm // d
                for r in range(d):
                    hp_ref[gi, r * p:(r + 1) * p, cs] = hf_ref[c, pl.ds(r, p, stride=d), :].astype(BF16)

    res = _dot(hp_ref[j // 3], w_ref[...]).astype(BF16)
    for gi, d in enumerate(dilations):
        @pl.when(j // 3 == gi)
        def _(gi=gi, d=d):
            if d == 1:
                o_refs[gi][...] = res
            else:
                p = tm // d
                for r in range(d):
                    o_refs[gi][r] = res[r * p:(r + 1) * p, :]


def _qkv_proj(x, g, w, bsz, s, dilations, tm, tn):
    m, k = x.shape
    ng = len(dilations)
    wq = w.shape[1] // ng
    assert wq == 3 * tn and s % tm == 0
    out_specs, out_shapes = [], []
    for gi, d in enumerate(dilations):
        col_of = lambda j, gi=gi: jnp.clip(j - 3 * gi, 0, 2)
        if d == 1:
            out_specs.append(pl.BlockSpec((tm, tn), lambda i, j, c=col_of: (i, c(j))))
            out_shapes.append(jax.ShapeDtypeStruct((m, wq), BF16))
        else:
            out_specs.append(_perm_block_spec(s, d, tm, tn, col_of))
            out_shapes.append(jax.ShapeDtypeStruct(_perm_view_shape(bsz, s, d, tm, wq), BF16))
    outs = pl.pallas_call(
        functools.partial(_qkv_proj_kernel, dilations=dilations),
        grid=(m // tm, 3 * ng),
        in_specs=[pl.BlockSpec((tm, k), lambda i, j: (i, 0)),
                  pl.BlockSpec((1, k), lambda i, j: (0, 0)),
                  pl.BlockSpec((k, tn), lambda i, j: (0, (j % 3) * ng + j // 3))],
        out_specs=out_specs,
        out_shape=out_shapes,
        scratch_shapes=[pltpu.VMEM((k // LANES, tm, LANES), F32), pltpu.VMEM((ng, tm, k), BF16)],
        compiler_params=_params("parallel", "arbitrary"),
        name="qkv_proj",
    )(x, g, w)
    return [o.reshape(bsz, s, wq) for o in outs]


def _inproj_kernel(x_ref, g_ref, w_ref, wdt_ref, dtb_ref, zx_ref, dt_ref, h_ref):
    @pl.when(pl.program_id(1) == 0)
    def _():
        h = _rms(x_ref[...], g_ref[...]).astype(BF16)
        h_ref[...] = h
        dt_ref[...] = _softplus(_dot(h, wdt_ref[...]) + dtb_ref[...])

    zx_ref[...] = _dot(h_ref[...], w_ref[...]).astype(zx_ref.dtype)


def _inproj(x, g, w_in, n, w_dt, dt_bias, tm, tn):
    m, k = x.shape
    assert n % tn == 0
    return pl.pallas_call(
        _inproj_kernel,
        grid=(m // tm, n // tn),
        in_specs=[pl.BlockSpec((tm, k), lambda i, j: (i, 0)),
                  pl.BlockSpec((1, k), lambda i, j: (0, 0)),
                  pl.BlockSpec((k, tn), lambda i, j: (0, j)),
                  pl.BlockSpec((k, LANES), lambda i, j: (0, 0)),
                  pl.BlockSpec((1, LANES), lambda i, j: (0, 0))],
        out_specs=[pl.BlockSpec((tm, tn), lambda i, j: (i, j)),
                   pl.BlockSpec((tm, LANES), lambda i, j: (i, 0))],
        out_shape=[jax.ShapeDtypeStruct((m, n), BF16),
                   jax.ShapeDtypeStruct((m, LANES), F32)],
        scratch_shapes=[pltpu.VMEM((tm, k), BF16)],
        compiler_params=_params("parallel", "arbitrary"),
        name="ssm_in_proj",
    )(x, g, w_in, w_dt, dt_bias)


def _matmul_res_kernel(a_ref, w_ref, r_ref, o_ref):
    o_ref[...] = r_ref[...] + _dot(a_ref[...], w_ref[...])


def _matmul_res(a, w, res, tm, tn):
    m, k = a.shape
    n = w.shape[1]
    return pl.pallas_call(
        _matmul_res_kernel,
        grid=(m // tm, n // tn),
        in_specs=[pl.BlockSpec((tm, k), lambda i, j: (i, 0)),
                  pl.BlockSpec((k, tn), lambda i, j: (0, j)),
                  pl.BlockSpec((tm, tn), lambda i, j: (i, j))],
        out_specs=pl.BlockSpec((tm, tn), lambda i, j: (i, j)),
        out_shape=jax.ShapeDtypeStruct((m, n), F32),
        compiler_params=_params("parallel", "arbitrary"),
        name="matmul_residual",
    )(a, w, res)


def _mlp_kernel(x_ref, g_ref, w1_ref, w2_ref, fg_ref, o_ref, h_ref, *, final_norm):
    f = pl.program_id(1)

    @pl.when(f == 0)
    def _():
        x = x_ref[...]
        h_ref[...] = _rms(x, g_ref[...]).astype(BF16)
        o_ref[...] = x

    a = jnp.maximum(_dot(h_ref[...], w1_ref[...]), 0.0)
    o_ref[...] += _dot((a * a).astype(BF16), w2_ref[...])

    if final_norm:
        @pl.when(f == pl.num_programs(1) - 1)
        def _():
            o_ref[...] = _rms(o_ref[...], fg_ref[...])


def _mlp(x, g, w1, w2, layer, fg, final_norm, tm, tf):
    m, d = x.shape
    ff = w1.shape[2]
    return pl.pallas_call(
        functools.partial(_mlp_kernel, final_norm=final_norm),
        grid=(m // tm, ff // tf),
        in_specs=[pl.BlockSpec((tm, d), lambda i, f: (i, 0)),
                  pl.BlockSpec((1, d), lambda i, f: (0, 0)),
                  pl.BlockSpec((None, d, tf), lambda i, f: (layer, 0, f)),
                  pl.BlockSpec((None, tf, d), lambda i, f: (layer, f, 0)),
                  pl.BlockSpec((1, d), lambda i, f: (0, 0))],
        out_specs=pl.BlockSpec((tm, d), lambda i, f: (i, 0)),
        out_shape=jax.ShapeDtypeStruct((m, d), F32),
        scratch_shapes=[pltpu.VMEM((tm, d), BF16)],
        compiler_params=_params("parallel", "arbitrary"),
        name="mlp",
    )(x, g, w1, w2, fg)


def _ssd_kernel(zx_ref, dt_ref, cw_ref, cb_ref, alog_ref, dsk_ref, nw_ref, e_ref, y_ref,
                ubuf_ref, xc_ref, state_ref, *, d_inner):
    q = CHUNK
    conv_dim = d_inner + 2 * SSM_GROUPS * SSM_STATE
    gw = d_inner // SSM_GROUPS
    hpg = gw // SSM_HEAD_DIM

    @pl.when(pl.program_id(1) == 0)
    def _():
        ubuf_ref[0:CARRY_ROWS, :] = jnp.zeros((CARRY_ROWS, conv_dim), F32)
        state_ref[...] = jnp.zeros_like(state_ref)

    cpiece = 512
    for c0 in range(0, conv_dim, cpiece):
        cs = slice(c0, c0 + cpiece)
        ubuf_ref[CARRY_ROWS:CARRY_ROWS + q, cs] = (
            zx_ref[0, :, d_inner + c0:d_inner + c0 + cpiece].astype(F32))
        acc = jnp.broadcast_to(cb_ref[:, cs], (q, cpiece))
        for k in range(CONV_WIDTH):
            r0 = CARRY_ROWS - (CONV_WIDTH - 1) + k
            acc = acc + cw_ref[k:k + 1, cs] * ubuf_ref[r0:r0 + q, cs]
        xc_ref[:, cs] = _silu(acc)
        ubuf_ref[0:CARRY_ROWS, cs] = ubuf_ref[q:q + CARRY_ROWS, cs]

    dt = dt_ref[0]
    a = dt * (-jnp.exp(alog_ref[...]))
    row = lax.broadcasted_iota(jnp.int32, (q, q), 0)
    col = lax.broadcasted_iota(jnp.int32, (q, q), 1)
    causal = row >= col
    tril = causal.astype(BF16)
    a_hi, a_mid, a_lo = _split3(a)
    a_cs = _dot(tril, a_hi) + _dot(tril, a_mid) + _dot(tril, a_lo)
    a_cs_t = a_cs.T
    a_last = a_cs[q - 1:q, :]

    e = e_ref[...]
    dt_x = _expand(dt, e)
    grow_x = _expand(jnp.exp(a_cs), e)
    dec_x = _expand(jnp.exp(a_last - a_cs), e)

    lane = lax.broadcasted_iota(jnp.int32, (q, LANES), 1)
    lo_half = lane < SSM_HEAD_DIM

    for g in range(SSM_GROUPS):
        gs = slice(g * gw, (g + 1) * gw)
        bg = xc_ref[:, d_inner + g * SSM_STATE:d_inner + (g + 1) * SSM_STATE].astype(BF16)
        cg = xc_ref[:, d_inner + (SSM_GROUPS + g) * SSM_STATE:
                    d_inner + (SSM_GROUPS + g + 1) * SSM_STATE].astype(BF16)
        cb = _dot_nt(cg, bg)
        xs = xc_ref[:, gs]
        xdt = xs * dt_x[:, gs]
        st = state_ref[:, gs]
        y = _dot(cg, st.astype(BF16)) * grow_x[:, gs]
        new_st = _dot_tn(bg, (xdt * dec_x[:, gs]).astype(BF16))
        state_ref[:, gs] = st * grow_x[q - 1:q, gs] + new_st

        ydiag = []
        for pr in range(hpg // 2):
            lhs = []
            for half in range(2):
                h = g * hpg + 2 * pr + half
                diff = a_cs[:, h:h + 1] - a_cs_t[h:h + 1, :]
                decay = jnp.exp(jnp.where(causal, diff, -jnp.inf))
                lhs.append((cb * decay).astype(BF16))
            xp = xdt[:, pr * LANES:(pr + 1) * LANES]
            rhs = jnp.concatenate([jnp.where(lo_half, xp, 0.0), jnp.where(lo_half, 0.0, xp)],
                                  axis=0).astype(BF16)
            ydiag.append(_dot(jnp.concatenate(lhs, axis=1), rhs))
        y = y + jnp.concatenate(ydiag, axis=1) + dsk_ref[:, gs] * xs

        gated = y * _silu(zx_ref[0, :, gs].astype(F32))
        yn = gated * lax.rsqrt(jnp.mean(gated * gated, axis=-1, keepdims=True) + EPS)
        y_ref[0, :, gs] = (yn * nw_ref[:, gs]).astype(y_ref.dtype)


def _ssd(zx, dt, conv_w, conv_b, a_log, d_skip_x, norm_w, expand, d_inner):
    bsz, s, n_zx = zx.shape
    conv_dim = n_zx - d_inner
    kern = functools.partial(_ssd_kernel, d_inner=d_inner)
    const = lambda b, c: (0, 0)
    return pl.pallas_call(
        kern,
        grid=(bsz, s // CHUNK),
        in_specs=[pl.BlockSpec((1, CHUNK, n_zx), lambda b, c: (b, c, 0)),
                  pl.BlockSpec((1, CHUNK, LANES), lambda b, c: (b, c, 0)),
                  pl.BlockSpec((CONV_WIDTH, conv_dim), const),
                  pl.BlockSpec((1, conv_dim), const),
                  pl.BlockSpec((1, LANES), const),
                  pl.BlockSpec((1, d_inner), const),
                  pl.BlockSpec((1, d_inner), const),
                  pl.BlockSpec((2 * LANES, d_inner), const)],
        out_specs=pl.BlockSpec((1, CHUNK, d_inner), lambda b, c: (b, c, 0)),
        out_shape=jax.ShapeDtypeStruct((bsz, s, d_inner), BF16),
        scratch_shapes=[pltpu.VMEM((CARRY_ROWS + CHUNK, conv_dim), F32),
                        pltpu.VMEM((CHUNK, conv_dim), F32),
                        pltpu.VMEM((SSM_STATE, d_inner), F32)],
        compiler_params=_params("parallel", "arbitrary"),
        name="ssd_chunk",
    )(zx, dt, conv_w, conv_b, a_log, d_skip_x, norm_w, expand)


def _attn_kernel(slopes_ref, q_ref, kp_ref, kc_ref, vp_ref, vc_ref, o_ref, lse_ref, *, dilation):
    blk = ATTN_BLOCK
    has_prev = pl.program_id(1) >= dilation
    qi = lax.broadcasted_iota(jnp.int32, (blk, 2 * blk), 0)
    kj = lax.broadcasted_iota(jnp.int32, (blk, 2 * blk), 1)
    dist = qi + blk - kj
    valid = (dist >= 0) & (dist <= blk) & (has_prev | (kj >= blk))
    neg_dist = jnp.where(valid, -(dist * dilation).astype(F32), -jnp.inf)
    lo_half = lax.broadcasted_iota(jnp.int32, (blk, LANES), 1) < ATTN_HEAD_DIM
    lo_half2 = lax.broadcasted_iota(jnp.int32, (2 * blk, LANES), 1) < ATTN_HEAD_DIM
    scale = 1.0 / math.sqrt(ATTN_HEAD_DIM)
    nil = jnp.zeros((2 * blk, LANES), BF16)
    row4 = lax.broadcasted_iota(jnp.int32, (4 * blk, LANES), 0)
    lane4 = lax.broadcasted_iota(jnp.int32, (4 * blk, LANES), 1)
    ones_cat = jnp.where((row4 < 2 * blk) == (lane4 < ATTN_HEAD_DIM), 1.0, 0.0).astype(BF16)

    for pr in range(HEADS_PER_GROUP // 2):
        ls = slice(pr * LANES, (pr + 1) * LANES)
        qp = q_ref[0, :, ls] * scale
        kk = jnp.concatenate([kp_ref[0, :, ls], kc_ref[0, :, ls]], axis=0)
        vv = jnp.concatenate([vp_ref[0, :, ls], vc_ref[0, :, ls]], axis=0)
        ps, ms = [], []
        for half in range(2):
            keep = lo_half if half == 0 else jnp.logical_not(lo_half)
            qh = jnp.where(keep, qp, jnp.zeros_like(qp))
            s = _dot_nt(qh, kk) + slopes_ref[2 * pr + half] * neg_dist
            m = jnp.max(s, axis=-1, keepdims=True)
            ps.append(jnp.exp(s - m).astype(BF16))
            ms.append(m)
        pcat = jnp.concatenate(ps, axis=1)
        vcat = jnp.concatenate([jnp.where(lo_half2, vv, nil), jnp.where(lo_half2, nil, vv)], axis=0)
        den = _dot(pcat, ones_cat)
        o_ref[0, :, ls] = _dot(pcat, vcat) * (1.0 / den)
        lse_ref[0, :, ls] = jnp.where(lo_half, ms[0], ms[1]) + jnp.log(den)


def _attn_group(qkv, slopes, g, dilation):
    bsz, s, _ = qkv.shape
    w = HEADS_PER_GROUP * ATTN_HEAD_DIM
    blk = (1, ATTN_BLOCK, w)

    def spec(which, prev):
        if prev:
            return pl.BlockSpec(blk, lambda b, n: (b, jnp.maximum(n - dilation, 0), which))
        return pl.BlockSpec(blk, lambda b, n: (b, n, which))

    out_spec = pl.BlockSpec(blk, lambda b, n: (b, n, 0))
    return pl.pallas_call(
        functools.partial(_attn_kernel, dilation=dilation),
        grid=(bsz, s // ATTN_BLOCK),
        in_specs=[pl.BlockSpec(memory_space=pltpu.SMEM),
                  spec(0, False), spec(1, True), spec(1, False), spec(2, True), spec(2, False)],
        out_specs=[out_spec, out_spec],
        out_shape=[jax.ShapeDtypeStruct((bsz, s, w), F32)] * 2,
        compiler_params=_params("parallel", "arbitrary"),
        name=f"dilated_attn_g{g}",
    )(slopes, qkv, qkv, qkv, qkv, qkv)


def _attn_out_kernel(*refs, dilations):
    ng = len(dilations)
    o_refs, l_refs = refs[:ng], refs[ng:2 * ng]
    w_ref, x_ref, out_ref, mix_ref = refs[2 * ng:2 * ng + 4]
    nat_refs = refs[2 * ng + 4:]
    tm = x_ref.shape[0]
    for c in range(mix_ref.shape[1] // LANES):
        cs = slice(c * LANES, (c + 1) * LANES)
        os_, ls_ = [], []
        k = 0
        for gi, d in enumerate(dilations):
            if d == 1:
                os_.append(o_refs[gi][:, cs])
                ls_.append(l_refs[gi][:, cs])
                continue
            p = tm // d
            for src, dst in ((o_refs[gi], nat_refs[k]), (l_refs[gi], nat_refs[k + 1])):
                for r in range(d):
                    dst[c, pl.ds(r, p, stride=d), :] = src[r, :, cs]
            os_.append(nat_refs[k][c])
            ls_.append(nat_refs[k + 1][c])
            k += 2
        m = functools.reduce(jnp.maximum, ls_)
        es = [jnp.exp(l - m) for l in ls_]
        mix = sum(e * o for e, o in zip(es, os_)) / sum(es)
        mix_ref[:, cs] = mix.astype(BF16)
    out_ref[...] = x_ref[...] + _dot(mix_ref[...], w_ref[...])


def _attn_out(os_, lses, w_o, x, bsz, s, dilations, tm):
    m, d_model = x.shape
    w = w_o.shape[0]
    specs, views = [], []
    for arrs in (os_, lses):
        for arr, d in zip(arrs, dilations):
            if d == 1:
                specs.append(pl.BlockSpec((tm, w), lambda i: (i, 0)))
                views.append(arr.reshape(m, w))
            else:
                specs.append(_perm_block_spec(s, d, tm, w, lambda: 0))
                views.append(arr.reshape(_perm_view_shape(bsz, s, d, tm, w)))
    n_perm = sum(d != 1 for d in dilations)
    return pl.pallas_call(
        functools.partial(_attn_out_kernel, dilations=dilations),
        grid=(m // tm,),
        in_specs=specs + [pl.BlockSpec((w, d_model), lambda i: (0, 0)),
                          pl.BlockSpec((tm, d_model), lambda i: (i, 0))],
        out_specs=pl.BlockSpec((tm, d_model), lambda i: (i, 0)),
        out_shape=jax.ShapeDtypeStruct((m, d_model), F32),
        scratch_shapes=[pltpu.VMEM((tm, w), BF16)] + [pltpu.VMEM((w // LANES, tm, LANES), F32)] * (2 * n_perm),
        compiler_params=_params("parallel"),
        name="attn_combine_out_proj",
    )(*views, w_o, x)


def _alibi_slopes(n_groups):
    n = n_groups * HEADS_PER_GROUP
    i = jnp.arange(1, n + 1, dtype=F32)
    return jnp.exp2(-8.0 * i / n).reshape(n_groups, HEADS_PER_GROUP)


def _mamba_layer(x2, bsz, s, g_mix, w_in, conv_w, conv_b, dt_bias, a_log, d_skip, norm_w, w_out):
    d_model = x2.shape[1]
    d_inner = w_out.shape[0]
    heads = d_inner // SSM_HEAD_DIM
    n_zx = w_in.shape[1] - heads
    assert heads <= LANES and s % CHUNK == 0
    w_in = w_in.astype(BF16)
    w_dt = jnp.pad(w_in[:, n_zx:], ((0, 0), (0, LANES - heads)))
    dtb = jnp.pad(dt_bias, (0, LANES - heads)).reshape(1, LANES)
    alog = jnp.pad(a_log, (0, LANES - heads)).reshape(1, LANES)
    zx, dt = _inproj(x2, g_mix.reshape(1, d_model), w_in, n_zx, w_dt, dtb, tm=1024, tn=1024)

    lane_head = jnp.arange(d_inner, dtype=jnp.int32) // SSM_HEAD_DIM
    expand = (jnp.arange(LANES, dtype=jnp.int32)[:, None] == lane_head[None, :]).astype(BF16)
    expand = jnp.concatenate([expand, expand], axis=0)
    d_skip_x = jnp.repeat(d_skip, SSM_HEAD_DIM).reshape(1, d_inner)
    y = _ssd(zx.reshape(bsz, s, n_zx), dt.reshape(bsz, s, LANES), conv_w,
             conv_b.reshape(1, -1), alog, d_skip_x, norm_w.reshape(1, d_inner), expand, d_inner)
    return _matmul_res(y.reshape(bsz * s, d_inner), w_out.astype(BF16), x2, tm=1024, tn=1024)


def _attn_layer(x2, bsz, s, g_mix, w_qkv, w_o):
    d_model = x2.shape[1]
    n_groups = len(ATTN_GROUPS)
    assert all(win // dil == ATTN_BLOCK and s % win == 0 for win, dil in ATTN_GROUPS)
    dilations = tuple(dil for _, dil in ATTN_GROUPS)
    w = HEADS_PER_GROUP * ATTN_HEAD_DIM
    qkvs = _qkv_proj(x2, g_mix.reshape(1, d_model), w_qkv.astype(BF16), bsz, s, dilations,
                     tm=512, tn=w)
    slopes = _alibi_slopes(n_groups)
    os_, lses = [], []
    for g, dilation in enumerate(dilations):
        o, lse = _attn_group(qkvs[g], slopes[g], g, dilation)
        os_.append(o)
        lses.append(lse)
    return _attn_out(os_, lses, w_o.astype(BF16), x2, bsz, s, dilations, tm=256)


def kernel(x, norm_mix, norm_mlp, ssm_w_in, ssm_conv_w, ssm_conv_b, ssm_dt_bias, ssm_a_log, ssm_d,
           ssm_norm_w, ssm_w_out, attn_w_qkv, attn_w_o, mlp_w1, mlp_w2, final_norm):
    bsz, s, d_model = x.shape
    depth = norm_mix.shape[0]
    x2 = x.reshape(bsz * s, d_model)
    fg = final_norm.reshape(1, d_model)
    w1, w2 = mlp_w1.astype(BF16), mlp_w2.astype(BF16)
    for i in range(depth):
        j = i // 2
        if i % 2 == 0:
            x2 = _mamba_layer(x2, bsz, s, norm_mix[i], ssm_w_in[j], ssm_conv_w[j], ssm_conv_b[j],
                              ssm_dt_bias[j], ssm_a_log[j], ssm_d[j], ssm_norm_w[j], ssm_w_out[j])
        else:
            x2 = _attn_layer(x2, bsz, s, norm_mix[i], attn_w_qkv[j], attn_w_o[j])
        x2 = _mlp(x2, norm_mlp[i].reshape(1, d_model), w1, w2, i, fg,
                  final_norm=(i == depth - 1), tm=1024, tf=512)
    return x2.reshape(bsz, s, d_model)
```

```python
import functools
import math

import jax
import jax.numpy as jnp
from jax import lax
from jax.experimental import pallas as pl
from jax.experimental.pallas import tpu as pltpu

F32 = jnp.float32
BF16 = jnp.bfloat16

EPS = 1e-5
SSM_HEAD_DIM = 64
SSM_GROUPS = 8
SSM_STATE = 128
CONV_WIDTH = 4
CHUNK = 128
ATTN_GROUPS = ((128, 1), (512, 4), (2048, 16))
HEADS_PER_GROUP = 16
ATTN_HEAD_DIM = 64
ATTN_BLOCK = 128
LANES = 128

VMEM_LIMIT = 56 * 1024 * 1024


def _params(*sem):
    return pltpu.CompilerParams(dimension_semantics=sem, vmem_limit_bytes=VMEM_LIMIT)


def _rms(x, g):
    return x * lax.rsqrt(jnp.mean(x * x, axis=-1, keepdims=True) + EPS) * g


def _silu(x):
    h = 0.5 * x
    return h + h * jnp.tanh(h)


def _softplus(v):
    return jnp.maximum(v, 0.0) + jnp.log1p(jnp.exp(-jnp.abs(v)))


def _split3(v):
    hi = v.astype(BF16)
    r = v - hi.astype(F32)
    mid = r.astype(BF16)
    lo = (r - mid.astype(F32)).astype(BF16)
    return hi, mid, lo


def _dot(a, b):
    return jnp.dot(a, b, preferred_element_type=F32)


def _dot_nt(a, b):
    return lax.dot_general(a, b, (((1,), (1,)), ((), ())), preferred_element_type=F32)


def _dot_tn(a, b):
    return lax.dot_general(a, b, (((0,), (0,)), ((), ())), preferred_element_type=F32)


def _expand(v, e2):
    hi = v.astype(BF16)
    lo = (v - hi.astype(F32)).astype(BF16)
    return _dot(jnp.concatenate([hi, lo], axis=1), e2)


def _perm_view_shape(bsz, s, dilation, tm, c):
    span = ATTN_BLOCK * dilation
    return (bsz, s // span, dilation, span // tm, tm // dilation, c)


def _perm_block_spec(s, dilation, tm, cblk, col_of):
    span = ATTN_BLOCK * dilation
    per_seq, per_span = s // tm, span // tm

    def index(i, *rest):
        ib = i % per_seq
        return (i // per_seq, ib // per_span, 0, ib % per_span, 0, col_of(*rest))

    return pl.BlockSpec((None, None, dilation, None, tm // dilation, cblk), index)


def _perm_pieces(tm, dilation):
    span = ATTN_BLOCK * dilation
    if tm >= span:
        return [(w0 + r * ATTN_BLOCK, w0 + r, ATTN_BLOCK)
                for w0 in range(0, tm, span) for r in range(dilation)]
    p = tm // dilation
    return [(r * p, r, p) for r in range(dilation)]


def _qkv_proj_kernel(x_ref, g_ref, w_ref, *refs, dilations):
    ng = len(dilations)
    o_refs, hf_ref, hp_ref = refs[:ng], refs[ng], refs[ng + 1]
    j = pl.program_id(1)
    tm, k = x_ref.shape

    @pl.when(j == 0)
    def _():
        x = x_ref[...]
        inv = lax.rsqrt(jnp.mean(x * x, axis=-1, keepdims=True) + EPS)
        for c in range(k // LANES):
            cs = slice(c * LANES, (c + 1) * LANES)
            slot = c % hf_ref.shape[0]
            h = x_ref[:, cs] * inv * g_ref[:, cs]
            hf_ref[slot] = h
            for gi, d in enumerate(dilations):
                if d == 1:
                    hp_ref[gi, :, cs] = h.astype(BF16)
                    continue
                for dst, src, p in _perm_pieces(tm, d):
                    hp_ref[gi, dst:dst + p, cs] = hf_ref[slot, pl.ds(src, p, stride=d), :].astype(BF16)

    res = _dot(hp_ref[j // 3], w_ref[...]).astype(BF16)
    for gi, d in enumerate(dilations):
        @pl.when(j // 3 == gi)
        def _(gi=gi, d=d):
            if len(o_refs[gi].shape) == 2:
                o_refs[gi][...] = res
            else:
                p = tm // d
                for r in range(d):
                    o_refs[gi][r] = res[r * p:(r + 1) * p, :]


def _qkv_proj(x, g, w, bsz, s, dilations, tm, tn):
    m, k = x.shape
    ng = len(dilations)
    wq = w.shape[1] // ng
    assert wq == 3 * tn and s % tm == 0
    out_specs, out_shapes = [], []
    for gi, d in enumerate(dilations):
        col_of = lambda j, gi=gi: jnp.clip(j - 3 * gi, 0, 2)
        if tm >= ATTN_BLOCK * d:
            out_specs.append(pl.BlockSpec((tm, tn), lambda i, j, c=col_of: (i, c(j))))
            out_shapes.append(jax.ShapeDtypeStruct((m, wq), BF16))
        else:
            out_specs.append(_perm_block_spec(s, d, tm, tn, col_of))
            out_shapes.append(jax.ShapeDtypeStruct(_perm_view_shape(bsz, s, d, tm, wq), BF16))
    outs = pl.pallas_call(
        functools.partial(_qkv_proj_kernel, dilations=dilations),
        grid=(m // tm, 3 * ng),
        in_specs=[pl.BlockSpec((tm, k), lambda i, j: (i, 0)),
                  pl.BlockSpec((1, k), lambda i, j: (0, 0)),
                  pl.BlockSpec((k, tn), lambda i, j: (0, (j % 3) * ng + j // 3))],
        out_specs=out_specs,
        out_shape=out_shapes,
        scratch_shapes=[pltpu.VMEM((2, tm, LANES), F32), pltpu.VMEM((ng, tm, k), BF16)],
        compiler_params=_params("parallel", "arbitrary"),
        name="qkv_proj",
    )(x, g, w)
    return [o.reshape(bsz, s, wq) for o in outs]


def _inproj_kernel(x_ref, g_ref, w_ref, wdt_ref, dtb_ref, zx_ref, dt_ref, h_ref):
    @pl.when(pl.program_id(1) == 0)
    def _():
        h = _rms(x_ref[...], g_ref[...]).astype(BF16)
        h_ref[...] = h
        dt_ref[...] = _softplus(_dot(h, wdt_ref[...]) + dtb_ref[...])

    zx_ref[...] = _dot(h_ref[...], w_ref[...]).astype(zx_ref.dtype)


def _inproj(x, g, w_in, n, w_dt, dt_bias, tm, tn):
    m, k = x.shape
    assert n % tn == 0
    return pl.pallas_call(
        _inproj_kernel,
        grid=(m // tm, n // tn),
        in_specs=[pl.BlockSpec((tm, k), lambda i, j: (i, 0)),
                  pl.BlockSpec((1, k), lambda i, j: (0, 0)),
                  pl.BlockSpec((k, tn), lambda i, j: (0, j)),
                  pl.BlockSpec((k, LANES), lambda i, j: (0, 0)),
                  pl.BlockSpec((1, LANES), lambda i, j: (0, 0))],
        out_specs=[pl.BlockSpec((tm, tn), lambda i, j: (i, j)),
                   pl.BlockSpec((tm, LANES), lambda i, j: (i, 0))],
        out_shape=[jax.ShapeDtypeStruct((m, n), BF16),
                   jax.ShapeDtypeStruct((m, LANES), F32)],
        scratch_shapes=[pltpu.VMEM((tm, k), BF16)],
        compiler_params=_params("parallel", "arbitrary"),
        name="ssm_in_proj",
    )(x, g, w_in, w_dt, dt_bias)


def _matmul_res_kernel(a_ref, w_ref, r_ref, o_ref):
    o_ref[...] = r_ref[...] + _dot(a_ref[...], w_ref[...])


def _matmul_res(a, w, res, tm, tn):
    m, k = a.shape
    n = w.shape[1]
    return pl.pallas_call(
        _matmul_res_kernel,
        grid=(m // tm, n // tn),
        in_specs=[pl.BlockSpec((tm, k), lambda i, j: (i, 0)),
                  pl.BlockSpec((k, tn), lambda i, j: (0, j)),
                  pl.BlockSpec((tm, tn), lambda i, j: (i, j))],
        out_specs=pl.BlockSpec((tm, tn), lambda i, j: (i, j)),
        out_shape=jax.ShapeDtypeStruct((m, n), F32),
        compiler_params=_params("parallel", "arbitrary"),
        name="matmul_residual",
    )(a, w, res)


def _mlp_kernel(x_ref, g_ref, w1_ref, w2_ref, fg_ref, o_ref, h_ref, *, final_norm):
    f = pl.program_id(1)

    @pl.when(f == 0)
    def _():
        x = x_ref[...]
        h_ref[...] = _rms(x, g_ref[...]).astype(BF16)
        o_ref[...] = x

    a = jnp.maximum(_dot(h_ref[...], w1_ref[...]), 0.0)
    o_ref[...] += _dot((a * a).astype(BF16), w2_ref[...])

    if final_norm:
        @pl.when(f == pl.num_programs(1) - 1)
        def _():
            o_ref[...] = _rms(o_ref[...], fg_ref[...])


def _mlp(x, g, w1, w2, layer, fg, final_norm, tm, tf):
    m, d = x.shape
    ff = w1.shape[2]
    return pl.pallas_call(
        functools.partial(_mlp_kernel, final_norm=final_norm),
        grid=(m // tm, ff // tf),
        in_specs=[pl.BlockSpec((tm, d), lambda i, f: (i, 0)),
                  pl.BlockSpec((1, d), lambda i, f: (0, 0)),
                  pl.BlockSpec((None, d, tf), lambda i, f: (layer, 0, f)),
                  pl.BlockSpec((None, tf, d), lambda i, f: (layer, f, 0)),
                  pl.BlockSpec((1, d), lambda i, f: (0, 0))],
        out_specs=pl.BlockSpec((tm, d), lambda i, f: (i, 0)),
        out_shape=jax.ShapeDtypeStruct((m, d), F32),
        scratch_shapes=[pltpu.VMEM((tm, d), BF16)],
        compiler_params=_params("parallel", "arbitrary"),
        name="mlp",
    )(x, g, w1, w2, fg)


def _ssd_kernel(zx_ref, dt_ref, cw_ref, cb_ref, alog_ref, dsk_ref, nw_ref, e_ref, shift_ref, y_ref,
                ucat_ref, xc_ref, state_ref, *, d_inner):
    q = CHUNK
    conv_dim = d_inner + 2 * SSM_GROUPS * SSM_STATE
    gw = d_inner // SSM_GROUPS
    hpg = gw // SSM_HEAD_DIM

    @pl.when(pl.program_id(1) == 0)
    def _():
        ucat_ref[0:q, :] = jnp.zeros((q, conv_dim), BF16)
        state_ref[...] = jnp.zeros_like(state_ref)

    cpiece = 512
    for c0 in range(0, conv_dim, cpiece):
        cs = slice(c0, c0 + cpiece)
        u = zx_ref[0, :, d_inner + c0:d_inner + c0 + cpiece]
        ucat_ref[q:2 * q, cs] = u
        back = _dot(shift_ref[...], ucat_ref[:, cs])
        acc = cb_ref[:, cs] + cw_ref[CONV_WIDTH - 1:CONV_WIDTH, cs] * u.astype(F32)
        for k in range(1, CONV_WIDTH):
            w_k = cw_ref[CONV_WIDTH - 1 - k:CONV_WIDTH - k, cs]
            acc = acc + w_k * back[(k - 1) * q:k * q]
        xc_ref[:, cs] = _silu(acc)
        ucat_ref[0:q, cs] = u

    dt = dt_ref[0]
    a = dt * (-jnp.exp(alog_ref[...]))
    row = lax.broadcasted_iota(jnp.int32, (q, q), 0)
    col = lax.broadcasted_iota(jnp.int32, (q, q), 1)
    causal = row >= col
    tril = causal.astype(BF16)
    a_hi, a_mid, a_lo = _split3(a)
    a_cs = _dot(tril, a_hi) + _dot(tril, a_mid) + _dot(tril, a_lo)
    a_cs_t = a_cs.T
    a_last = a_cs[q - 1:q, :]

    e = e_ref[...]
    dt_x = _expand(dt, e)
    grow_x = _expand(jnp.exp(a_cs), e)
    dec_x = _expand(jnp.exp(a_last - a_cs), e)

    lane = lax.broadcasted_iota(jnp.int32, (q, LANES), 1)
    lo_half = lane < SSM_HEAD_DIM

    for g in range(SSM_GROUPS):
        gs = slice(g * gw, (g + 1) * gw)
        bg = xc_ref[:, d_inner + g * SSM_STATE:d_inner + (g + 1) * SSM_STATE].astype(BF16)
        cg = xc_ref[:, d_inner + (SSM_GROUPS + g) * SSM_STATE:
                    d_inner + (SSM_GROUPS + g + 1) * SSM_STATE].astype(BF16)
        cb = _dot_nt(cg, bg)
        xs = xc_ref[:, gs]
        xdt = xs * dt_x[:, gs]
        st = state_ref[:, gs]
        y = _dot(cg, st.astype(BF16)) * grow_x[:, gs]
        new_st = _dot_tn(bg, (xdt * dec_x[:, gs]).astype(BF16))
        state_ref[:, gs] = st * grow_x[q - 1:q, gs] + new_st

        ydiag = []
        for pr in range(hpg // 2):
            lhs = []
            for half in range(2):
                h = g * hpg + 2 * pr + half
                diff = a_cs[:, h:h + 1] - a_cs_t[h:h + 1, :]
                decay = jnp.exp(jnp.where(causal, diff, -jnp.inf))
                lhs.append((cb * decay).astype(BF16))
            xp = xdt[:, pr * LANES:(pr + 1) * LANES]
            rhs = jnp.concatenate([jnp.where(lo_half, xp, 0.0), jnp.where(lo_half, 0.0, xp)],
                                  axis=0).astype(BF16)
            ydiag.append(_dot(jnp.concatenate(lhs, axis=1), rhs))
        y = y + jnp.concatenate(ydiag, axis=1) + dsk_ref[:, gs] * xs

        gated = y * _silu(zx_ref[0, :, gs].astype(F32))
        yn = gated * lax.rsqrt(jnp.mean(gated * gated, axis=-1, keepdims=True) + EPS)
        y_ref[0, :, gs] = (yn * nw_ref[:, gs]).astype(y_ref.dtype)


def _ssd(zx, dt, conv_w, conv_b, a_log, d_skip_x, norm_w, expand, d_inner):
    bsz, s, n_zx = zx.shape
    conv_dim = n_zx - d_inner
    kern = functools.partial(_ssd_kernel, d_inner=d_inner)
    const = lambda b, c: (0, 0)
    rows = jnp.arange((CONV_WIDTH - 1) * CHUNK, dtype=jnp.int32)
    src = CHUNK + rows % CHUNK - (rows // CHUNK + 1)
    shift = (jnp.arange(2 * CHUNK, dtype=jnp.int32)[None, :] == src[:, None]).astype(BF16)
    return pl.pallas_call(
        kern,
        grid=(bsz, s // CHUNK),
        in_specs=[pl.BlockSpec((1, CHUNK, n_zx), lambda b, c: (b, c, 0)),
                  pl.BlockSpec((1, CHUNK, LANES), lambda b, c: (b, c, 0)),
                  pl.BlockSpec((CONV_WIDTH, conv_dim), const),
                  pl.BlockSpec((1, conv_dim), const),
                  pl.BlockSpec((1, LANES), const),
                  pl.BlockSpec((1, d_inner), const),
                  pl.BlockSpec((1, d_inner), const),
                  pl.BlockSpec((2 * LANES, d_inner), const),
                  pl.BlockSpec(((CONV_WIDTH - 1) * CHUNK, 2 * CHUNK), const)],
        out_specs=pl.BlockSpec((1, CHUNK, d_inner), lambda b, c: (b, c, 0)),
        out_shape=jax.ShapeDtypeStruct((bsz, s, d_inner), BF16),
        scratch_shapes=[pltpu.VMEM((2 * CHUNK, conv_dim), BF16),
                        pltpu.VMEM((CHUNK, conv_dim), F32),
                        pltpu.VMEM((SSM_STATE, d_inner), F32)],
        compiler_params=_params("parallel", "arbitrary"),
        name="ssd_chunk",
    )(zx, dt, conv_w, conv_b, a_log, d_skip_x, norm_w, expand, shift)


def _attn_kernel(slopes_ref, q_ref, kp_ref, kc_ref, vp_ref, vc_ref, o_ref, lse_ref, s_ref, p_ref, *,
                 dilation):
    blk = ATTN_BLOCK
    has_prev = pl.program_id(1) >= dilation
    qi = lax.broadcasted_iota(jnp.int32, (blk, 2 * blk), 0)
    kj = lax.broadcasted_iota(jnp.int32, (blk, 2 * blk), 1)
    dist = qi + blk - kj
    valid = (dist >= 0) & (dist <= blk) & (has_prev | (kj >= blk))
    neg_dist = jnp.where(valid, -(dist * dilation).astype(F32), -jnp.inf)
    lo_half = lax.broadcasted_iota(jnp.int32, (blk, LANES), 1) < ATTN_HEAD_DIM
    lo_half2 = lax.broadcasted_iota(jnp.int32, (2 * blk, LANES), 1) < ATTN_HEAD_DIM
    scale = 1.0 / math.sqrt(ATTN_HEAD_DIM)
    nil = jnp.zeros((2 * blk, LANES), BF16)
    row4 = lax.broadcasted_iota(jnp.int32, (4 * blk, LANES), 0)
    lane4 = lax.broadcasted_iota(jnp.int32, (4 * blk, LANES), 1)
    ones_cat = jnp.where((row4 < 2 * blk) == (lane4 < ATTN_HEAD_DIM), 1.0, 0.0).astype(BF16)

    n_pairs = HEADS_PER_GROUP // 2
    qnil = jnp.zeros((blk, LANES), BF16)
    for pr in range(n_pairs):
        ls = slice(pr * LANES, (pr + 1) * LANES)
        qp = q_ref[0, :, ls] * scale
        q2 = jnp.concatenate([jnp.where(lo_half, qp, qnil), jnp.where(lo_half, qnil, qp)], axis=0)
        kk = jnp.concatenate([kp_ref[0, :, ls], kc_ref[0, :, ls]], axis=0)
        s2 = _dot_nt(q2, kk)
        s_ref[pr, 0:blk, :] = s2[0:blk] + slopes_ref[2 * pr] * neg_dist
        s_ref[pr, blk:2 * blk, :] = s2[blk:2 * blk] + slopes_ref[2 * pr + 1] * neg_dist
    for pr in range(n_pairs):
        ls = slice(pr * LANES, (pr + 1) * LANES)
        s2 = s_ref[pr]
        m = jnp.max(s2, axis=-1, keepdims=True)
        p = jnp.exp(s2 - m).astype(BF16)
        p_ref[pr] = jnp.concatenate([p[0:blk], p[blk:2 * blk]], axis=1)
        lse_ref[0, :, ls] = jnp.where(lo_half, m[0:blk], m[blk:2 * blk])
    for pr in range(n_pairs):
        ls = slice(pr * LANES, (pr + 1) * LANES)
        vv = jnp.concatenate([vp_ref[0, :, ls], vc_ref[0, :, ls]], axis=0)
        vcat = jnp.concatenate([jnp.where(lo_half2, vv, nil), jnp.where(lo_half2, nil, vv)], axis=0)
        od = _dot(p_ref[pr], jnp.concatenate([vcat, ones_cat], axis=1))
        den = od[:, LANES:]
        o_ref[0, :, ls] = od[:, :LANES] * (1.0 / den)
        lse_ref[0, :, ls] += jnp.log(den)


def _attn_group(qkv, slopes, g, dilation):
    bsz, s, _ = qkv.shape
    w = HEADS_PER_GROUP * ATTN_HEAD_DIM
    blk = (1, ATTN_BLOCK, w)

    def spec(which, prev):
        if prev:
            return pl.BlockSpec(blk, lambda b, n: (b, jnp.maximum(n - dilation, 0), which))
        return pl.BlockSpec(blk, lambda b, n: (b, n, which))

    out_spec = pl.BlockSpec(blk, lambda b, n: (b, n, 0))
    return pl.pallas_call(
        functools.partial(_attn_kernel, dilation=dilation),
        grid=(bsz, s // ATTN_BLOCK),
        in_specs=[pl.BlockSpec(memory_space=pltpu.SMEM),
                  spec(0, False), spec(1, True), spec(1, False), spec(2, True), spec(2, False)],
        out_specs=[out_spec, out_spec],
        out_shape=[jax.ShapeDtypeStruct((bsz, s, w), F32)] * 2,
        scratch_shapes=[pltpu.VMEM((HEADS_PER_GROUP // 2, 2 * ATTN_BLOCK, 2 * ATTN_BLOCK), F32),
                        pltpu.VMEM((HEADS_PER_GROUP // 2, ATTN_BLOCK, 4 * ATTN_BLOCK), BF16)],
        compiler_params=_params("parallel", "arbitrary"),
        name=f"dilated_attn_g{g}",
    )(slopes, qkv, qkv, qkv, qkv, qkv)


def _attn_out_kernel(*refs, dilations):
    ng = len(dilations)
    o_refs, l_refs = refs[:ng], refs[ng:2 * ng]
    w_ref, x_ref, out_ref, mix_ref = refs[2 * ng:2 * ng + 4]
    nat_refs = refs[2 * ng + 4:]
    tm = x_ref.shape[0]
    for c in range(mix_ref.shape[1] // LANES):
        cs = slice(c * LANES, (c + 1) * LANES)
        os_, ls_ = [], []
        k = 0
        for gi, d in enumerate(dilations):
            if d == 1:
                os_.append(o_refs[gi][:, cs])
                ls_.append(l_refs[gi][:, cs])
                continue
            p = tm // d
            for src, dst in ((o_refs[gi], nat_refs[k]), (l_refs[gi], nat_refs[k + 1])):
                for r in range(d):
                    dst[c, pl.ds(r, p, stride=d), :] = src[r, :, cs]
            os_.append(nat_refs[k][c])
            ls_.append(nat_refs[k + 1][c])
            k += 2
        m = functools.reduce(jnp.maximum, ls_)
        es = [jnp.exp(l - m) for l in ls_]
        mix = sum(e * o for e, o in zip(es, os_)) / sum(es)
        mix_ref[:, cs] = mix.astype(BF16)
    out_ref[...] = x_ref[...] + _dot(mix_ref[...], w_ref[...])


def _attn_out(os_, lses, w_o, x, bsz, s, dilations, tm):
    m, d_model = x.shape
    w = w_o.shape[0]
    specs, views = [], []
    for arrs in (os_, lses):
        for arr, d in zip(arrs, dilations):
            if d == 1:
                specs.append(pl.BlockSpec((tm, w), lambda i: (i, 0)))
                views.append(arr.reshape(m, w))
            else:
                specs.append(_perm_block_spec(s, d, tm, w, lambda: 0))
                views.append(arr.reshape(_perm_view_shape(bsz, s, d, tm, w)))
    n_perm = sum(d != 1 for d in dilations)
    return pl.pallas_call(
        functools.partial(_attn_out_kernel, dilations=dilations),
        grid=(m // tm,),
        in_specs=specs + [pl.BlockSpec((w, d_model), lambda i: (0, 0)),
                          pl.BlockSpec((tm, d_model), lambda i: (i, 0))],
        out_specs=pl.BlockSpec((tm, d_model), lambda i: (i, 0)),
        out_shape=jax.ShapeDtypeStruct((m, d_model), F32),
        scratch_shapes=[pltpu.VMEM((tm, w), BF16)] + [pltpu.VMEM((w // LANES, tm, LANES), F32)] * (2 * n_perm),
        compiler_params=_params("parallel"),
        name="attn_combine_out_proj",
    )(*views, w_o, x)


def _alibi_slopes(n_groups):
    n = n_groups * HEADS_PER_GROUP
    i = jnp.arange(1, n + 1, dtype=F32)
    return jnp.exp2(-8.0 * i / n).reshape(n_groups, HEADS_PER_GROUP)


def _mamba_layer(x2, bsz, s, g_mix, w_in, conv_w, conv_b, dt_bias, a_log, d_skip, norm_w, w_out):
    d_model = x2.shape[1]
    d_inner = w_out.shape[0]
    heads = d_inner // SSM_HEAD_DIM
    n_zx = w_in.shape[1] - heads
    assert heads <= LANES and s % CHUNK == 0
    w_in = w_in.astype(BF16)
    w_dt = jnp.pad(w_in[:, n_zx:], ((0, 0), (0, LANES - heads)))
    dtb = jnp.pad(dt_bias, (0, LANES - heads)).reshape(1, LANES)
    alog = jnp.pad(a_log, (0, LANES - heads)).reshape(1, LANES)
    zx, dt = _inproj(x2, g_mix.reshape(1, d_model), w_in, n_zx, w_dt, dtb, tm=1024, tn=2048)

    lane_head = jnp.arange(d_inner, dtype=jnp.int32) // SSM_HEAD_DIM
    expand = (jnp.arange(LANES, dtype=jnp.int32)[:, None] == lane_head[None, :]).astype(BF16)
    expand = jnp.concatenate([expand, expand], axis=0)
    d_skip_x = jnp.repeat(d_skip, SSM_HEAD_DIM).reshape(1, d_inner)
    y = _ssd(zx.reshape(bsz, s, n_zx), dt.reshape(bsz, s, LANES), conv_w,
             conv_b.reshape(1, -1), alog, d_skip_x, norm_w.reshape(1, d_inner), expand, d_inner)
    return _matmul_res(y.reshape(bsz * s, d_inner), w_out.astype(BF16), x2, tm=1024, tn=1024)


def _attn_layer(x2, bsz, s, g_mix, w_qkv, w_o):
    d_model = x2.shape[1]
    n_groups = len(ATTN_GROUPS)
    assert all(win // dil == ATTN_BLOCK and s % win == 0 for win, dil in ATTN_GROUPS)
    dilations = tuple(dil for _, dil in ATTN_GROUPS)
    w = HEADS_PER_GROUP * ATTN_HEAD_DIM
    qkvs = _qkv_proj(x2, g_mix.reshape(1, d_model), w_qkv.astype(BF16), bsz, s, dilations,
                     tm=1024, tn=w)
    slopes = _alibi_slopes(n_groups)
    os_, lses = [], []
    for g, dilation in enumerate(dilations):
        o, lse = _attn_group(qkvs[g], slopes[g], g, dilation)
        os_.append(o)
        lses.append(lse)
    return _attn_out(os_, lses, w_o.astype(BF16), x2, bsz, s, dilations, tm=256)


def kernel(x, norm_mix, norm_mlp, ssm_w_in, ssm_conv_w, ssm_conv_b, ssm_dt_bias, ssm_a_log, ssm_d,
           ssm_norm_w, ssm_w_out, attn_w_qkv, attn_w_o, mlp_w1, mlp_w2, final_norm):
    bsz, s, d_model = x.shape
    depth = norm_mix.shape[0]
    x2 = x.reshape(bsz * s, d_model)
    fg = final_norm.reshape(1, d_model)
    w1, w2 = mlp_w1.astype(BF16), mlp_w2.astype(BF16)
    for i in range(depth):
        j = i // 2
        if i % 2 == 0:
            x2 = _mamba_layer(x2, bsz, s, norm_mix[i], ssm_w_in[j], ssm_conv_w[j], ssm_conv_b[j],
                              ssm_dt_bias[j], ssm_a_log[j], ssm_d[j], ssm_norm_w[j], ssm_w_out[j])
        else:
            x2 = _attn_layer(x2, bsz, s, norm_mix[i], attn_w_qkv[j], attn_w_o[j])
        x2 = _mlp(x2, norm_mlp[i].reshape(1, d_model), w1, w2, i, fg,
                  final_norm=(i == depth - 1), tm=1024, tf=512)
    return x2.reshape(bsz, s, d_model)
```

```python
import functools
import math

import jax
import jax.numpy as jnp
from jax import lax
from jax.experimental import pallas as pl
from jax.experimental.pallas import tpu as pltpu

F32 = jnp.float32
BF16 = jnp.bfloat16

EPS = 1e-5
SSM_HEAD_DIM = 64
SSM_GROUPS = 8
SSM_STATE = 128
CONV_WIDTH = 4
CHUNK = 128
ATTN_GROUPS = ((128, 1), (512, 4), (2048, 16))
HEADS_PER_GROUP = 16
ATTN_HEAD_DIM = 64
ATTN_BLOCK = 128
LANES = 128

VMEM_LIMIT = 56 * 1024 * 1024


def _params(*sem):
    return pltpu.CompilerParams(dimension_semantics=sem, vmem_limit_bytes=VMEM_LIMIT)


def _rms(x, g):
    return x * lax.rsqrt(jnp.mean(x * x, axis=-1, keepdims=True) + EPS) * g


def _silu(x):
    h = 0.5 * x
    return h + h * jnp.tanh(h)


def _softplus(v):
    return jnp.maximum(v, 0.0) + jnp.log1p(jnp.exp(-jnp.abs(v)))


def _split3(v):
    hi = v.astype(BF16)
    r = v - hi.astype(F32)
    mid = r.astype(BF16)
    lo = (r - mid.astype(F32)).astype(BF16)
    return hi, mid, lo


def _dot(a, b):
    return jnp.dot(a, b, preferred_element_type=F32)


def _dot_nt(a, b):
    return lax.dot_general(a, b, (((1,), (1,)), ((), ())), preferred_element_type=F32)


def _dot_tn(a, b):
    return lax.dot_general(a, b, (((0,), (0,)), ((), ())), preferred_element_type=F32)


def _expand(v, e2):
    hi = v.astype(BF16)
    lo = (v - hi.astype(F32)).astype(BF16)
    return _dot(jnp.concatenate([hi, lo], axis=1), e2)


def _perm_view_shape(bsz, s, dilation, tm, c):
    span = ATTN_BLOCK * dilation
    return (bsz, s // span, dilation, span // tm, tm // dilation, c)


def _perm_block_spec(s, dilation, tm, cblk, col_of):
    span = ATTN_BLOCK * dilation
    per_seq, per_span = s // tm, span // tm

    def index(i, *rest):
        ib = i % per_seq
        return (i // per_seq, ib // per_span, 0, ib % per_span, 0, col_of(*rest))

    return pl.BlockSpec((None, None, dilation, None, tm // dilation, cblk), index)


def _perm_pieces(tm, dilation):
    span = ATTN_BLOCK * dilation
    if tm >= span:
        return [(w0 + r * ATTN_BLOCK, w0 + r, ATTN_BLOCK)
                for w0 in range(0, tm, span) for r in range(dilation)]
    p = tm // dilation
    return [(r * p, r, p) for r in range(dilation)]


def _qkv_proj_kernel(x_ref, g_ref, w_ref, *refs, dilations):
    ng = len(dilations)
    o_refs, hf_ref, hp_ref = refs[:ng], refs[ng], refs[ng + 1]
    j = pl.program_id(1)
    tm, k = x_ref.shape

    @pl.when(j == 0)
    def _():
        x = x_ref[...]
        inv = lax.rsqrt(jnp.mean(x * x, axis=-1, keepdims=True) + EPS)
        for c in range(k // LANES):
            cs = slice(c * LANES, (c + 1) * LANES)
            slot = c % hf_ref.shape[0]
            h = x_ref[:, cs] * inv * g_ref[:, cs]
            hf_ref[slot] = h
            for gi, d in enumerate(dilations):
                if d == 1:
                    hp_ref[gi, :, cs] = h.astype(BF16)
                    continue
                for dst, src, p in _perm_pieces(tm, d):
                    hp_ref[gi, dst:dst + p, cs] = hf_ref[slot, pl.ds(src, p, stride=d), :].astype(BF16)

    res = _dot(hp_ref[j // 3], w_ref[...]).astype(BF16)
    for gi, d in enumerate(dilations):
        @pl.when(j // 3 == gi)
        def _(gi=gi, d=d):
            if len(o_refs[gi].shape) == 2:
                o_refs[gi][...] = res
            else:
                p = tm // d
                for r in range(d):
                    o_refs[gi][r] = res[r * p:(r + 1) * p, :]


def _qkv_proj(x, g, w, bsz, s, dilations, tm, tn):
    m, k = x.shape
    ng = len(dilations)
    wq = w.shape[1] // ng
    assert wq == 3 * tn and s % tm == 0
    out_specs, out_shapes = [], []
    for gi, d in enumerate(dilations):
        col_of = lambda j, gi=gi: jnp.clip(j - 3 * gi, 0, 2)
        if tm >= ATTN_BLOCK * d:
            out_specs.append(pl.BlockSpec((tm, tn), lambda i, j, c=col_of: (i, c(j))))
            out_shapes.append(jax.ShapeDtypeStruct((m, wq), BF16))
        else:
            out_specs.append(_perm_block_spec(s, d, tm, tn, col_of))
            out_shapes.append(jax.ShapeDtypeStruct(_perm_view_shape(bsz, s, d, tm, wq), BF16))
    outs = pl.pallas_call(
        functools.partial(_qkv_proj_kernel, dilations=dilations),
        grid=(m // tm, 3 * ng),
        in_specs=[pl.BlockSpec((tm, k), lambda i, j: (i, 0)),
                  pl.BlockSpec((1, k), lambda i, j: (0, 0)),
                  pl.BlockSpec((k, tn), lambda i, j: (0, (j % 3) * ng + j // 3))],
        out_specs=out_specs,
        out_shape=out_shapes,
        scratch_shapes=[pltpu.VMEM((2, tm, LANES), F32), pltpu.VMEM((ng, tm, k), BF16)],
        compiler_params=_params("parallel", "arbitrary"),
        name="qkv_proj",
    )(x, g, w)
    return [o.reshape(bsz, s, wq) for o in outs]


def _inproj_kernel(x_ref, g_ref, w_ref, wdt_ref, dtb_ref, zx_ref, dt_ref, h_ref):
    @pl.when(pl.program_id(1) == 0)
    def _():
        h = _rms(x_ref[...], g_ref[...]).astype(BF16)
        h_ref[...] = h
        dt_ref[...] = _softplus(_dot(h, wdt_ref[...]) + dtb_ref[...])

    zx_ref[...] = _dot(h_ref[...], w_ref[...]).astype(zx_ref.dtype)


def _inproj(x, g, w_in, n, w_dt, dt_bias, tm, tn):
    m, k = x.shape
    assert n % tn == 0
    return pl.pallas_call(
        _inproj_kernel,
        grid=(m // tm, n // tn),
        in_specs=[pl.BlockSpec((tm, k), lambda i, j: (i, 0)),
                  pl.BlockSpec((1, k), lambda i, j: (0, 0)),
                  pl.BlockSpec((k, tn), lambda i, j: (0, j)),
                  pl.BlockSpec((k, LANES), lambda i, j: (0, 0)),
                  pl.BlockSpec((1, LANES), lambda i, j: (0, 0))],
        out_specs=[pl.BlockSpec((tm, tn), lambda i, j: (i, j)),
                   pl.BlockSpec((tm, LANES), lambda i, j: (i, 0))],
        out_shape=[jax.ShapeDtypeStruct((m, n), BF16),
                   jax.ShapeDtypeStruct((m, LANES), F32)],
        scratch_shapes=[pltpu.VMEM((tm, k), BF16)],
        compiler_params=_params("parallel", "arbitrary"),
        name="ssm_in_proj",
    )(x, g, w_in, w_dt, dt_bias)


def _matmul_res_kernel(a_ref, w_ref, r_ref, o_ref):
    o_ref[...] = r_ref[...] + _dot(a_ref[...], w_ref[...])


def _matmul_res(a, w, res, tm, tn):
    m, k = a.shape
    n = w.shape[1]
    return pl.pallas_call(
        _matmul_res_kernel,
        grid=(m // tm, n // tn),
        in_specs=[pl.BlockSpec((tm, k), lambda i, j: (i, 0)),
                  pl.BlockSpec((k, tn), lambda i, j: (0, j)),
                  pl.BlockSpec((tm, tn), lambda i, j: (i, j))],
        out_specs=pl.BlockSpec((tm, tn), lambda i, j: (i, j)),
        out_shape=jax.ShapeDtypeStruct((m, n), F32),
        compiler_params=_params("parallel", "arbitrary"),
        name="matmul_residual",
    )(a, w, res)


def _mlp_kernel(x_ref, g_ref, w1_ref, w2_ref, fg_ref, o_ref, h_ref, *, final_norm):
    f = pl.program_id(1)

    @pl.when(f == 0)
    def _():
        x = x_ref[...]
        h_ref[...] = _rms(x, g_ref[...]).astype(BF16)
        o_ref[...] = x

    a = jnp.maximum(_dot(h_ref[...], w1_ref[...]), 0.0)
    o_ref[...] += _dot((a * a).astype(BF16), w2_ref[...])

    if final_norm:
        @pl.when(f == pl.num_programs(1) - 1)
        def _():
            o_ref[...] = _rms(o_ref[...], fg_ref[...])


def _mlp(x, g, w1, w2, layer, fg, final_norm, tm, tf):
    m, d = x.shape
    ff = w1.shape[2]
    return pl.pallas_call(
        functools.partial(_mlp_kernel, final_norm=final_norm),
        grid=(m // tm, ff // tf),
        in_specs=[pl.BlockSpec((tm, d), lambda i, f: (i, 0)),
                  pl.BlockSpec((1, d), lambda i, f: (0, 0)),
                  pl.BlockSpec((None, d, tf), lambda i, f: (layer, 0, f)),
                  pl.BlockSpec((None, tf, d), lambda i, f: (layer, f, 0)),
                  pl.BlockSpec((1, d), lambda i, f: (0, 0))],
        out_specs=pl.BlockSpec((tm, d), lambda i, f: (i, 0)),
        out_shape=jax.ShapeDtypeStruct((m, d), F32),
        scratch_shapes=[pltpu.VMEM((tm, d), BF16)],
        compiler_params=_params("parallel", "arbitrary"),
        name="mlp",
    )(x, g, w1, w2, fg)


def _ssd_kernel(zx_ref, dt_ref, cw_ref, cb_ref, alog_ref, dsk_ref, nw_ref, e_ref, shift_ref, y_ref,
                ucat_ref, xc_ref, state_ref, *, d_inner):
    q = CHUNK
    conv_dim = d_inner + 2 * SSM_GROUPS * SSM_STATE
    gw = d_inner // SSM_GROUPS
    hpg = gw // SSM_HEAD_DIM

    @pl.when(pl.program_id(1) == 0)
    def _():
        ucat_ref[0:q, :] = jnp.zeros((q, conv_dim), BF16)
        state_ref[...] = jnp.zeros_like(state_ref)

    cpiece = 512
    for c0 in range(0, conv_dim, cpiece):
        cs = slice(c0, c0 + cpiece)
        u = zx_ref[0, :, d_inner + c0:d_inner + c0 + cpiece]
        ucat_ref[q:2 * q, cs] = u
        back = _dot(shift_ref[...], ucat_ref[:, cs])
        acc = cb_ref[:, cs] + cw_ref[CONV_WIDTH - 1:CONV_WIDTH, cs] * u.astype(F32)
        for k in range(1, CONV_WIDTH):
            w_k = cw_ref[CONV_WIDTH - 1 - k:CONV_WIDTH - k, cs]
            acc = acc + w_k * back[(k - 1) * q:k * q]
        xc_ref[:, cs] = _silu(acc)
        ucat_ref[0:q, cs] = u

    dt = dt_ref[0]
    a = dt * (-jnp.exp(alog_ref[...]))
    row = lax.broadcasted_iota(jnp.int32, (q, q), 0)
    col = lax.broadcasted_iota(jnp.int32, (q, q), 1)
    causal = row >= col
    tril = causal.astype(BF16)
    a_hi, a_mid, a_lo = _split3(a)
    a_cs = _dot(tril, a_hi) + _dot(tril, a_mid) + _dot(tril, a_lo)
    a_cs_t = a_cs.T
    a_last = a_cs[q - 1:q, :]

    e = e_ref[...]
    dt_x = _expand(dt, e)
    grow_x = _expand(jnp.exp(a_cs), e)
    dec_x = _expand(jnp.exp(a_last - a_cs), e)

    lane = lax.broadcasted_iota(jnp.int32, (q, LANES), 1)
    lo_half = lane < SSM_HEAD_DIM

    for g in range(SSM_GROUPS):
        gs = slice(g * gw, (g + 1) * gw)
        bg = xc_ref[:, d_inner + g * SSM_STATE:d_inner + (g + 1) * SSM_STATE].astype(BF16)
        cg = xc_ref[:, d_inner + (SSM_GROUPS + g) * SSM_STATE:
                    d_inner + (SSM_GROUPS + g + 1) * SSM_STATE].astype(BF16)
        cb = _dot_nt(cg, bg)
        xs = xc_ref[:, gs]
        xdt = xs * dt_x[:, gs]
        st = state_ref[:, gs]
        y = _dot(cg, st.astype(BF16)) * grow_x[:, gs]
        new_st = _dot_tn(bg, (xdt * dec_x[:, gs]).astype(BF16))
        state_ref[:, gs] = st * grow_x[q - 1:q, gs] + new_st

        ydiag = []
        for pr in range(hpg // 2):
            lhs = []
            for half in range(2):
                h = g * hpg + 2 * pr + half
                diff = a_cs[:, h:h + 1] - a_cs_t[h:h + 1, :]
                decay = jnp.exp(jnp.where(causal, diff, -jnp.inf))
                lhs.append((cb * decay).astype(BF16))
            xp = xdt[:, pr * LANES:(pr + 1) * LANES]
            rhs = jnp.concatenate([jnp.where(lo_half, xp, 0.0), jnp.where(lo_half, 0.0, xp)],
                                  axis=0).astype(BF16)
            ydiag.append(_dot(jnp.concatenate(lhs, axis=1), rhs))
        y = y + jnp.concatenate(ydiag, axis=1) + dsk_ref[:, gs] * xs

        gated = y * _silu(zx_ref[0, :, gs].astype(F32))
        yn = gated * lax.rsqrt(jnp.mean(gated * gated, axis=-1, keepdims=True) + EPS)
        y_ref[0, :, gs] = (yn * nw_ref[:, gs]).astype(y_ref.dtype)


def _ssd(zx, dt, conv_w, conv_b, a_log, d_skip_x, norm_w, expand, d_inner):
    bsz, s, n_zx = zx.shape
    conv_dim = n_zx - d_inner
    kern = functools.partial(_ssd_kernel, d_inner=d_inner)
    const = lambda b, c: (0, 0)
    rows = jnp.arange((CONV_WIDTH - 1) * CHUNK, dtype=jnp.int32)
    src = CHUNK + rows % CHUNK - (rows // CHUNK + 1)
    shift = (jnp.arange(2 * CHUNK, dtype=jnp.int32)[None, :] == src[:, None]).astype(BF16)
    return pl.pallas_call(
        kern,
        grid=(bsz, s // CHUNK),
        in_specs=[pl.BlockSpec((1, CHUNK, n_zx), lambda b, c: (b, c, 0)),
                  pl.BlockSpec((1, CHUNK, LANES), lambda b, c: (b, c, 0)),
                  pl.BlockSpec((CONV_WIDTH, conv_dim), const),
                  pl.BlockSpec((1, conv_dim), const),
                  pl.BlockSpec((1, LANES), const),
                  pl.BlockSpec((1, d_inner), const),
                  pl.BlockSpec((1, d_inner), const),
                  pl.BlockSpec((2 * LANES, d_inner), const),
                  pl.BlockSpec(((CONV_WIDTH - 1) * CHUNK, 2 * CHUNK), const)],
        out_specs=pl.BlockSpec((1, CHUNK, d_inner), lambda b, c: (b, c, 0)),
        out_shape=jax.ShapeDtypeStruct((bsz, s, d_inner), BF16),
        scratch_shapes=[pltpu.VMEM((2 * CHUNK, conv_dim), BF16),
                        pltpu.VMEM((CHUNK, conv_dim), F32),
                        pltpu.VMEM((SSM_STATE, d_inner), F32)],
        compiler_params=_params("parallel", "arbitrary"),
        name="ssd_chunk",
    )(zx, dt, conv_w, conv_b, a_log, d_skip_x, norm_w, expand, shift)


def _attn_fused_kernel(slopes_ref, *refs, dilations, pairs):
    ng = len(dilations)
    ins, out_ref = refs[:5 * ng], refs[5 * ng]
    s_ref, p_ref, onat_ref, lnat_ref = refs[5 * ng + 1:]
    blk = ATTN_BLOCK
    sp, hb = pl.program_id(1), pl.program_id(2)
    nblk = out_ref.shape[1] // blk

    qi = lax.broadcasted_iota(jnp.int32, (blk, 2 * blk), 0)
    kj = lax.broadcasted_iota(jnp.int32, (blk, 2 * blk), 1)
    dist = qi + blk - kj
    band = (dist >= 0) & (dist <= blk)
    cur_only = band & (kj >= blk)
    negs = []
    for d in dilations:
        nd = -(dist * d).astype(F32)
        negs.append((jnp.where(band, nd, -jnp.inf), jnp.where(cur_only, nd, -jnp.inf)))
    lo_half = lax.broadcasted_iota(jnp.int32, (blk, LANES), 1) < ATTN_HEAD_DIM
    lo_half2 = lax.broadcasted_iota(jnp.int32, (2 * blk, LANES), 1) < ATTN_HEAD_DIM
    scale = 1.0 / math.sqrt(ATTN_HEAD_DIM)
    nil = jnp.zeros((2 * blk, LANES), BF16)
    qnil = jnp.zeros((blk, LANES), BF16)
    row4 = lax.broadcasted_iota(jnp.int32, (4 * blk, LANES), 0)
    lane4 = lax.broadcasted_iota(jnp.int32, (4 * blk, LANES), 1)
    ones_cat = jnp.where((row4 < 2 * blk) == (lane4 < ATTN_HEAD_DIM), 1.0, 0.0).astype(BF16)

    def history(i, d, cur_ref, prev_ref, ls):
        if d >= nblk:
            return prev_ref[0, pl.ds(pl.multiple_of(i * blk, blk), blk), ls]
        inside = cur_ref[0, pl.ds(pl.multiple_of(jnp.maximum(i - d, 0) * blk, blk), blk), ls]
        before = prev_ref[0, pl.ds(pl.multiple_of(jnp.minimum(i, d - 1) * blk, blk), blk), ls]
        return jnp.where(i >= d, inside, before)

    def block(i, carry):
        rows = pl.ds(pl.multiple_of(i * blk, blk), blk)
        for gi, d in enumerate(dilations):
            q_ref, k_ref, _, kp_ref, _ = ins[5 * gi:5 * gi + 5]
            has_prev = (i >= d) | (sp > 0)
            neg = jnp.where(has_prev, negs[gi][0], negs[gi][1])
            for pp in range(pairs):
                ls = slice(pp * LANES, (pp + 1) * LANES)
                u = gi * pairs + pp
                qp = q_ref[0, rows, ls] * scale
                q2 = jnp.concatenate([jnp.where(lo_half, qp, qnil), jnp.where(lo_half, qnil, qp)], axis=0)
                kk = jnp.concatenate([history(i, d, k_ref, kp_ref, ls), k_ref[0, rows, ls]], axis=0)
                s2 = _dot_nt(q2, kk)
                h0 = (hb * pairs + pp) * 2
                s_ref[u, 0:blk, :] = s2[0:blk] + slopes_ref[gi, h0] * neg
                s_ref[u, blk:2 * blk, :] = s2[blk:2 * blk] + slopes_ref[gi, h0 + 1] * neg
        ms = []
        for u in range(ng * pairs):
            s2 = s_ref[u]
            m = jnp.max(s2, axis=-1, keepdims=True)
            p = jnp.exp(s2 - m).astype(BF16)
            p_ref[u] = jnp.concatenate([p[0:blk], p[blk:2 * blk]], axis=1)
            ms.append(m)
        for gi, d in enumerate(dilations):
            _, _, v_ref, _, vp_ref = ins[5 * gi:5 * gi + 5]
            if d == 1:
                nat_rows = rows
            else:
                nat_rows = pl.ds((i // d) * (blk * d) + i % d, blk, stride=d)
            for pp in range(pairs):
                ls = slice(pp * LANES, (pp + 1) * LANES)
                u = gi * pairs + pp
                vv = jnp.concatenate([history(i, d, v_ref, vp_ref, ls), v_ref[0, rows, ls]], axis=0)
                vcat = jnp.concatenate([jnp.where(lo_half2, vv, nil), jnp.where(lo_half2, nil, vv)], axis=0)
                od = _dot(p_ref[u], jnp.concatenate([vcat, ones_cat], axis=1))
                den = od[:, LANES:]
                onat_ref[u, nat_rows, :] = od[:, :LANES] * (1.0 / den)
                lnat_ref[u, nat_rows, :] = (jnp.where(lo_half, ms[u][0:blk], ms[u][blk:2 * blk])
                                            + jnp.log(den))
        return carry

    lax.fori_loop(0, nblk, block, 0)

    def combine(c, carry):
        rows = pl.ds(pl.multiple_of(c * blk, blk), blk)
        for pp in range(pairs):
            ls_ = [lnat_ref[gi * pairs + pp, rows, :] for gi in range(ng)]
            m = functools.reduce(jnp.maximum, ls_)
            es = [jnp.exp(l - m) for l in ls_]
            mix = sum(e * onat_ref[gi * pairs + pp, rows, :] for gi, e in enumerate(es)) / sum(es)
            out_ref[0, rows, pp * LANES:(pp + 1) * LANES] = mix.astype(out_ref.dtype)
        return carry

    lax.fori_loop(0, nblk, combine, 0)


def _attn_fused(qkvs, slopes, bsz, s, dilations, pairs):
    w = HEADS_PER_GROUP * ATTN_HEAD_DIM
    cw = pairs * LANES
    ncb = w // cw
    span = ATTN_BLOCK * max(dilations)
    in_specs, args = [pl.BlockSpec(memory_space=pltpu.SMEM)], [slopes]
    for g, d in enumerate(dilations):
        for part in range(3):
            in_specs.append(pl.BlockSpec((1, span, cw),
                                         lambda b, sp, hb, part=part: (b, sp, part * ncb + hb)))
        hist_rows = ATTN_BLOCK * d
        per_span = span // hist_rows
        for part in (1, 2):
            in_specs.append(pl.BlockSpec(
                (1, hist_rows, cw),
                lambda b, sp, hb, part=part, ps=per_span: (b, jnp.maximum(sp * ps - 1, 0), part * ncb + hb)))
        args += [qkvs[g]] * 5
    units = len(dilations) * pairs
    return pl.pallas_call(
        functools.partial(_attn_fused_kernel, dilations=dilations, pairs=pairs),
        grid=(bsz, s // span, ncb),
        in_specs=in_specs,
        out_specs=pl.BlockSpec((1, span, cw), lambda b, sp, hb: (b, sp, hb)),
        out_shape=jax.ShapeDtypeStruct((bsz, s, w), BF16),
        scratch_shapes=[pltpu.VMEM((units, 2 * ATTN_BLOCK, 2 * ATTN_BLOCK), F32),
                        pltpu.VMEM((units, ATTN_BLOCK, 4 * ATTN_BLOCK), BF16),
                        pltpu.VMEM((units, span, LANES), F32),
                        pltpu.VMEM((units, span, LANES), F32)],
        compiler_params=_params("parallel", "parallel", "arbitrary"),
        name="dilated_attn_fused",
    )(*args)


def _alibi_slopes(n_groups):
    n = n_groups * HEADS_PER_GROUP
    i = jnp.arange(1, n + 1, dtype=F32)
    return jnp.exp2(-8.0 * i / n).reshape(n_groups, HEADS_PER_GROUP)


def _mamba_layer(x2, bsz, s, g_mix, w_in, conv_w, conv_b, dt_bias, a_log, d_skip, norm_w, w_out):
    d_model = x2.shape[1]
    d_inner = w_out.shape[0]
    heads = d_inner // SSM_HEAD_DIM
    n_zx = w_in.shape[1] - heads
    assert heads <= LANES and s % CHUNK == 0
    w_in = w_in.astype(BF16)
    w_dt = jnp.pad(w_in[:, n_zx:], ((0, 0), (0, LANES - heads)))
    dtb = jnp.pad(dt_bias, (0, LANES - heads)).reshape(1, LANES)
    alog = jnp.pad(a_log, (0, LANES - heads)).reshape(1, LANES)
    zx, dt = _inproj(x2, g_mix.reshape(1, d_model), w_in, n_zx, w_dt, dtb, tm=1024, tn=2048)

    lane_head = jnp.arange(d_inner, dtype=jnp.int32) // SSM_HEAD_DIM
    expand = (jnp.arange(LANES, dtype=jnp.int32)[:, None] == lane_head[None, :]).astype(BF16)
    expand = jnp.concatenate([expand, expand], axis=0)
    d_skip_x = jnp.repeat(d_skip, SSM_HEAD_DIM).reshape(1, d_inner)
    y = _ssd(zx.reshape(bsz, s, n_zx), dt.reshape(bsz, s, LANES), conv_w,
             conv_b.reshape(1, -1), alog, d_skip_x, norm_w.reshape(1, d_inner), expand, d_inner)
    return _matmul_res(y.reshape(bsz * s, d_inner), w_out.astype(BF16), x2, tm=1024, tn=1024)


def _attn_layer(x2, bsz, s, g_mix, w_qkv, w_o):
    d_model = x2.shape[1]
    n_groups = len(ATTN_GROUPS)
    assert all(win // dil == ATTN_BLOCK and s % win == 0 for win, dil in ATTN_GROUPS)
    dilations = tuple(dil for _, dil in ATTN_GROUPS)
    w = HEADS_PER_GROUP * ATTN_HEAD_DIM
    qkvs = _qkv_proj(x2, g_mix.reshape(1, d_model), w_qkv.astype(BF16), bsz, s, dilations,
                     tm=1024, tn=w)
    mix = _attn_fused(qkvs, _alibi_slopes(n_groups), bsz, s, dilations, pairs=2)
    return _matmul_res(mix.reshape(bsz * s, w), w_o.astype(BF16), x2, tm=1024, tn=1024)


def kernel(x, norm_mix, norm_mlp, ssm_w_in, ssm_conv_w, ssm_conv_b, ssm_dt_bias, ssm_a_log, ssm_d,
           ssm_norm_w, ssm_w_out, attn_w_qkv, attn_w_o, mlp_w1, mlp_w2, final_norm):
    bsz, s, d_model = x.shape
    depth = norm_mix.shape[0]
    x2 = x.reshape(bsz * s, d_model)
    fg = final_norm.reshape(1, d_model)
    w1, w2 = mlp_w1.astype(BF16), mlp_w2.astype(BF16)
    for i in range(depth):
        j = i // 2
        if i % 2 == 0:
            x2 = _mamba_layer(x2, bsz, s, norm_mix[i], ssm_w_in[j], ssm_conv_w[j], ssm_conv_b[j],
                              ssm_dt_bias[j], ssm_a_log[j], ssm_d[j], ssm_norm_w[j], ssm_w_out[j])
        else:
            x2 = _attn_layer(x2, bsz, s, norm_mix[i], attn_w_qkv[j], attn_w_o[j])
        x2 = _mlp(x2, norm_mlp[i].reshape(1, d_model), w1, w2, i, fg,
                  final_norm=(i == depth - 1), tm=1024, tf=512)
    return x2.reshape(bsz, s, d_model)
```

```python
import functools
import math

import jax
import jax.numpy as jnp
from jax import lax
from jax.experimental import pallas as pl
from jax.experimental.pallas import tpu as pltpu

F32 = jnp.float32
BF16 = jnp.bfloat16

EPS = 1e-5
SSM_HEAD_DIM = 64
SSM_GROUPS = 8
SSM_STATE = 128
CONV_WIDTH = 4
CHUNK = 128
ATTN_GROUPS = ((128, 1), (512, 4), (2048, 16))
HEADS_PER_GROUP = 16
ATTN_HEAD_DIM = 64
ATTN_BLOCK = 128
LANES = 128

VMEM_LIMIT = 60 * 1024 * 1024


def _params(*sem):
    return pltpu.CompilerParams(dimension_semantics=sem, vmem_limit_bytes=VMEM_LIMIT)


def _rms(x, g):
    return x * lax.rsqrt(jnp.mean(x * x, axis=-1, keepdims=True) + EPS) * g


def _silu(x):
    h = 0.5 * x
    return h + h * jnp.tanh(h)


def _softplus(v):
    return jnp.maximum(v, 0.0) + jnp.log1p(jnp.exp(-jnp.abs(v)))


def _split3(v):
    hi = v.astype(BF16)
    r = v - hi.astype(F32)
    mid = r.astype(BF16)
    lo = (r - mid.astype(F32)).astype(BF16)
    return hi, mid, lo


def _dot(a, b):
    return jnp.dot(a, b, preferred_element_type=F32)


def _dot_nt(a, b):
    return lax.dot_general(a, b, (((1,), (1,)), ((), ())), preferred_element_type=F32)


def _dot_tn(a, b):
    return lax.dot_general(a, b, (((0,), (0,)), ((), ())), preferred_element_type=F32)


def _expand(v, e2):
    hi = v.astype(BF16)
    lo = (v - hi.astype(F32)).astype(BF16)
    return _dot(jnp.concatenate([hi, lo], axis=1), e2)


def _perm_view_shape(bsz, s, dilation, tm, c):
    span = ATTN_BLOCK * dilation
    return (bsz, s // span, dilation, span // tm, tm // dilation, c)


def _perm_block_spec(s, dilation, tm, cblk, col_of):
    span = ATTN_BLOCK * dilation
    per_seq, per_span = s // tm, span // tm

    def index(i, *rest):
        ib = i % per_seq
        return (i // per_seq, ib // per_span, 0, ib % per_span, 0, col_of(*rest))

    return pl.BlockSpec((None, None, dilation, None, tm // dilation, cblk), index)


def _perm_pieces(tm, dilation):
    span = ATTN_BLOCK * dilation
    if tm >= span:
        return [(w0 + r * ATTN_BLOCK, w0 + r, ATTN_BLOCK)
                for w0 in range(0, tm, span) for r in range(dilation)]
    p = tm // dilation
    return [(r * p, r, p) for r in range(dilation)]


def _qkv_proj_kernel(x_ref, g_ref, w_ref, *refs, dilations):
    ng = len(dilations)
    o_refs, hf_ref, hp_ref = refs[:ng], refs[ng], refs[ng + 1]
    j = pl.program_id(1)
    tm, k = x_ref.shape

    @pl.when(j == 0)
    def _():
        x = x_ref[...]
        inv = lax.rsqrt(jnp.mean(x * x, axis=-1, keepdims=True) + EPS)
        for c in range(k // LANES):
            cs = slice(c * LANES, (c + 1) * LANES)
            slot = c % hf_ref.shape[0]
            h = x_ref[:, cs] * inv * g_ref[:, cs]
            hf_ref[slot] = h
            for gi, d in enumerate(dilations):
                if d == 1:
                    hp_ref[gi, :, cs] = h.astype(BF16)
                    continue
                for dst, src, p in _perm_pieces(tm, d):
                    hp_ref[gi, dst:dst + p, cs] = hf_ref[slot, pl.ds(src, p, stride=d), :].astype(BF16)

    res = _dot(hp_ref[j // 3], w_ref[...]).astype(BF16)
    for gi, d in enumerate(dilations):
        @pl.when(j // 3 == gi)
        def _(gi=gi, d=d):
            if len(o_refs[gi].shape) == 2:
                o_refs[gi][...] = res
            else:
                p = tm // d
                for r in range(d):
                    o_refs[gi][r] = res[r * p:(r + 1) * p, :]


def _qkv_proj(x, g, w, bsz, s, dilations, tm, tn):
    m, k = x.shape
    ng = len(dilations)
    wq = w.shape[1] // ng
    assert wq == 3 * tn and s % tm == 0
    out_specs, out_shapes = [], []
    for gi, d in enumerate(dilations):
        col_of = lambda j, gi=gi: jnp.clip(j - 3 * gi, 0, 2)
        if tm >= ATTN_BLOCK * d:
            out_specs.append(pl.BlockSpec((tm, tn), lambda i, j, c=col_of: (i, c(j))))
            out_shapes.append(jax.ShapeDtypeStruct((m, wq), BF16))
        else:
            out_specs.append(_perm_block_spec(s, d, tm, tn, col_of))
            out_shapes.append(jax.ShapeDtypeStruct(_perm_view_shape(bsz, s, d, tm, wq), BF16))
    outs = pl.pallas_call(
        functools.partial(_qkv_proj_kernel, dilations=dilations),
        grid=(m // tm, 3 * ng),
        in_specs=[pl.BlockSpec((tm, k), lambda i, j: (i, 0)),
                  pl.BlockSpec((1, k), lambda i, j: (0, 0)),
                  pl.BlockSpec((k, tn), lambda i, j: (0, (j % 3) * ng + j // 3))],
        out_specs=out_specs,
        out_shape=out_shapes,
        scratch_shapes=[pltpu.VMEM((2, tm, LANES), F32), pltpu.VMEM((ng, tm, k), BF16)],
        compiler_params=_params("parallel", "arbitrary"),
        name="qkv_proj",
    )(x, g, w)
    return [o.reshape(bsz, s, wq) for o in outs]


def _inproj_kernel(x_ref, g_ref, w_ref, wdt_ref, dtb_ref, zx_ref, dt_ref, h_ref):
    @pl.when(pl.program_id(1) == 0)
    def _():
        h = _rms(x_ref[...], g_ref[...]).astype(BF16)
        h_ref[...] = h
        dt_ref[...] = _softplus(_dot(h, wdt_ref[...]) + dtb_ref[...])

    zx_ref[...] = _dot(h_ref[...], w_ref[...]).astype(zx_ref.dtype)


def _inproj(x, g, w_in, n, w_dt, dt_bias, tm, tn):
    m, k = x.shape
    assert n % tn == 0
    return pl.pallas_call(
        _inproj_kernel,
        grid=(m // tm, n // tn),
        in_specs=[pl.BlockSpec((tm, k), lambda i, j: (i, 0)),
                  pl.BlockSpec((1, k), lambda i, j: (0, 0)),
                  pl.BlockSpec((k, tn), lambda i, j: (0, j)),
                  pl.BlockSpec((k, LANES), lambda i, j: (0, 0)),
                  pl.BlockSpec((1, LANES), lambda i, j: (0, 0))],
        out_specs=[pl.BlockSpec((tm, tn), lambda i, j: (i, j)),
                   pl.BlockSpec((tm, LANES), lambda i, j: (i, 0))],
        out_shape=[jax.ShapeDtypeStruct((m, n), BF16),
                   jax.ShapeDtypeStruct((m, LANES), F32)],
        scratch_shapes=[pltpu.VMEM((tm, k), BF16)],
        compiler_params=_params("parallel", "arbitrary"),
        name="ssm_in_proj",
    )(x, g, w_in, w_dt, dt_bias)


def _matmul_res_kernel(a_ref, w_ref, r_ref, o_ref):
    o_ref[...] = r_ref[...] + _dot(a_ref[...], w_ref[...])


def _matmul_res(a, w, res, tm, tn):
    m, k = a.shape
    n = w.shape[1]
    return pl.pallas_call(
        _matmul_res_kernel,
        grid=(m // tm, n // tn),
        in_specs=[pl.BlockSpec((tm, k), lambda i, j: (i, 0)),
                  pl.BlockSpec((k, tn), lambda i, j: (0, j)),
                  pl.BlockSpec((tm, tn), lambda i, j: (i, j))],
        out_specs=pl.BlockSpec((tm, tn), lambda i, j: (i, j)),
        out_shape=jax.ShapeDtypeStruct((m, n), F32),
        compiler_params=_params("parallel", "arbitrary"),
        name="matmul_residual",
    )(a, w, res)


def _mlp_kernel(x_ref, g_ref, w1_ref, w2_ref, fg_ref, o_ref, h_ref, *, final_norm):
    f = pl.program_id(1)

    @pl.when(f == 0)
    def _():
        x = x_ref[...]
        h_ref[...] = _rms(x, g_ref[...]).astype(BF16)
        o_ref[...] = x

    a = jnp.maximum(_dot(h_ref[...], w1_ref[...]), 0.0)
    o_ref[...] += _dot((a * a).astype(BF16), w2_ref[...])

    if final_norm:
        @pl.when(f == pl.num_programs(1) - 1)
        def _():
            o_ref[...] = _rms(o_ref[...], fg_ref[...])


def _mlp(x, g, w1, w2, layer, fg, final_norm, tm, tf):
    m, d = x.shape
    ff = w1.shape[2]
    return pl.pallas_call(
        functools.partial(_mlp_kernel, final_norm=final_norm),
        grid=(m // tm, ff // tf),
        in_specs=[pl.BlockSpec((tm, d), lambda i, f: (i, 0)),
                  pl.BlockSpec((1, d), lambda i, f: (0, 0)),
                  pl.BlockSpec((None, d, tf), lambda i, f: (layer, 0, f)),
                  pl.BlockSpec((None, tf, d), lambda i, f: (layer, f, 0)),
                  pl.BlockSpec((1, d), lambda i, f: (0, 0))],
        out_specs=pl.BlockSpec((tm, d), lambda i, f: (i, 0)),
        out_shape=jax.ShapeDtypeStruct((m, d), F32),
        scratch_shapes=[pltpu.VMEM((tm, d), BF16)],
        compiler_params=_params("parallel", "arbitrary"),
        name="mlp",
    )(x, g, w1, w2, fg)


def _ssd_kernel(zx_ref, dt_ref, cw_ref, cb_ref, alog_ref, dsk_ref, nw_ref, e_ref, shift_ref, y_ref,
                ucat_ref, xc_ref, state_ref, *, d_inner):
    @pl.when(pl.program_id(1) == 0)
    def _():
        ucat_ref[0:CHUNK, :] = jnp.zeros((CHUNK, ucat_ref.shape[1]), BF16)
        state_ref[...] = jnp.zeros_like(state_ref)

    def chunk(ci, carry):
        rows = pl.ds(pl.multiple_of(ci * CHUNK, CHUNK), CHUNK)
        _ssd_chunk(zx_ref.at[0, rows], dt_ref.at[0, rows], cw_ref, cb_ref, alog_ref, dsk_ref, nw_ref,
                   e_ref, shift_ref, y_ref.at[0, rows], ucat_ref, xc_ref, state_ref, d_inner=d_inner)
        return carry

    lax.fori_loop(0, zx_ref.shape[1] // CHUNK, chunk, 0)


def _ssd_chunk(zx_ref, dt_ref, cw_ref, cb_ref, alog_ref, dsk_ref, nw_ref, e_ref, shift_ref, y_ref,
               ucat_ref, xc_ref, state_ref, *, d_inner):
    q = CHUNK
    conv_dim = d_inner + 2 * SSM_GROUPS * SSM_STATE
    gw = d_inner // SSM_GROUPS
    hpg = gw // SSM_HEAD_DIM

    cpiece = 512
    for c0 in range(0, conv_dim, cpiece):
        cs = slice(c0, c0 + cpiece)
        u = zx_ref[:, d_inner + c0:d_inner + c0 + cpiece]
        ucat_ref[q:2 * q, cs] = u
        back = _dot(shift_ref[...], ucat_ref[:, cs])
        acc = cb_ref[:, cs] + cw_ref[CONV_WIDTH - 1:CONV_WIDTH, cs] * u.astype(F32)
        for k in range(1, CONV_WIDTH):
            w_k = cw_ref[CONV_WIDTH - 1 - k:CONV_WIDTH - k, cs]
            acc = acc + w_k * back[(k - 1) * q:k * q]
        xc_ref[:, cs] = _silu(acc)
        ucat_ref[0:q, cs] = u

    dt = dt_ref[...]
    a = dt * (-jnp.exp(alog_ref[...]))
    row = lax.broadcasted_iota(jnp.int32, (q, q), 0)
    col = lax.broadcasted_iota(jnp.int32, (q, q), 1)
    causal = row >= col
    tril = causal.astype(BF16)
    a_hi, a_mid, a_lo = _split3(a)
    a_cs = _dot(tril, a_hi) + _dot(tril, a_mid) + _dot(tril, a_lo)
    a_cs_t = a_cs.T
    a_last = a_cs[q - 1:q, :]

    e = e_ref[...]
    dt_x = _expand(dt, e)
    grow_x = _expand(jnp.exp(a_cs), e)
    dec_x = _expand(jnp.exp(a_last - a_cs), e)

    lane = lax.broadcasted_iota(jnp.int32, (q, LANES), 1)
    lo_half = lane < SSM_HEAD_DIM

    for g in range(SSM_GROUPS):
        gs = slice(g * gw, (g + 1) * gw)
        bg = xc_ref[:, d_inner + g * SSM_STATE:d_inner + (g + 1) * SSM_STATE].astype(BF16)
        cg = xc_ref[:, d_inner + (SSM_GROUPS + g) * SSM_STATE:
                    d_inner + (SSM_GROUPS + g + 1) * SSM_STATE].astype(BF16)
        cb = _dot_nt(cg, bg)
        xs = xc_ref[:, gs]
        xdt = xs * dt_x[:, gs]
        st = state_ref[:, gs]
        y = _dot(cg, st.astype(BF16)) * grow_x[:, gs]
        new_st = _dot_tn(bg, (xdt * dec_x[:, gs]).astype(BF16))
        state_ref[:, gs] = st * grow_x[q - 1:q, gs] + new_st

        ydiag = []
        for pr in range(hpg // 2):
            lhs = []
            for half in range(2):
                h = g * hpg + 2 * pr + half
                diff = a_cs[:, h:h + 1] - a_cs_t[h:h + 1, :]
                decay = jnp.exp(jnp.where(causal, diff, -jnp.inf))
                lhs.append((cb * decay).astype(BF16))
            xp = xdt[:, pr * LANES:(pr + 1) * LANES]
            rhs = jnp.concatenate([jnp.where(lo_half, xp, 0.0), jnp.where(lo_half, 0.0, xp)],
                                  axis=0).astype(BF16)
            ydiag.append(_dot(jnp.concatenate(lhs, axis=1), rhs))
        y = y + jnp.concatenate(ydiag, axis=1) + dsk_ref[:, gs] * xs

        gated = y * _silu(zx_ref[:, gs].astype(F32))
        yn = gated * lax.rsqrt(jnp.mean(gated * gated, axis=-1, keepdims=True) + EPS)
        y_ref[:, gs] = (yn * nw_ref[:, gs]).astype(y_ref.dtype)


def _ssd(zx, dt, conv_w, conv_b, a_log, d_skip_x, norm_w, expand, d_inner, chunks_per_step):
    bsz, s, n_zx = zx.shape
    conv_dim = n_zx - d_inner
    rows_per_step = chunks_per_step * CHUNK
    assert s % rows_per_step == 0
    kern = functools.partial(_ssd_kernel, d_inner=d_inner)
    const = lambda b, c: (0, 0)
    rows = jnp.arange((CONV_WIDTH - 1) * CHUNK, dtype=jnp.int32)
    src = CHUNK + rows % CHUNK - (rows // CHUNK + 1)
    shift = (jnp.arange(2 * CHUNK, dtype=jnp.int32)[None, :] == src[:, None]).astype(BF16)
    return pl.pallas_call(
        kern,
        grid=(bsz, s // rows_per_step),
        in_specs=[pl.BlockSpec((1, rows_per_step, n_zx), lambda b, c: (b, c, 0)),
                  pl.BlockSpec((1, rows_per_step, LANES), lambda b, c: (b, c, 0)),
                  pl.BlockSpec((CONV_WIDTH, conv_dim), const),
                  pl.BlockSpec((1, conv_dim), const),
                  pl.BlockSpec((1, LANES), const),
                  pl.BlockSpec((1, d_inner), const),
                  pl.BlockSpec((1, d_inner), const),
                  pl.BlockSpec((2 * LANES, d_inner), const),
                  pl.BlockSpec(((CONV_WIDTH - 1) * CHUNK, 2 * CHUNK), const)],
        out_specs=pl.BlockSpec((1, rows_per_step, d_inner), lambda b, c: (b, c, 0)),
        out_shape=jax.ShapeDtypeStruct((bsz, s, d_inner), BF16),
        scratch_shapes=[pltpu.VMEM((2 * CHUNK, conv_dim), BF16),
                        pltpu.VMEM((CHUNK, conv_dim), F32),
                        pltpu.VMEM((SSM_STATE, d_inner), F32)],
        compiler_params=_params("parallel", "arbitrary"),
        name="ssd_chunk",
    )(zx, dt, conv_w, conv_b, a_log, d_skip_x, norm_w, expand, shift)


def _attn_fused_kernel(slopes_ref, *refs, dilations, pairs):
    ng = len(dilations)
    ins, out_ref = refs[:5 * ng], refs[5 * ng]
    s_ref, p_ref, onat_ref, lnat_ref = refs[5 * ng + 1:]
    blk = ATTN_BLOCK
    sp, hb = pl.program_id(1), pl.program_id(2)
    nblk = out_ref.shape[1] // blk

    qi = lax.broadcasted_iota(jnp.int32, (blk, 2 * blk), 0)
    kj = lax.broadcasted_iota(jnp.int32, (blk, 2 * blk), 1)
    dist = qi + blk - kj
    band = (dist >= 0) & (dist <= blk)
    cur_only = band & (kj >= blk)
    negs = []
    for d in dilations:
        nd = -(dist * d).astype(F32)
        negs.append((jnp.where(band, nd, -jnp.inf), jnp.where(cur_only, nd, -jnp.inf)))
    lo_half = lax.broadcasted_iota(jnp.int32, (blk, LANES), 1) < ATTN_HEAD_DIM
    lo_half2 = lax.broadcasted_iota(jnp.int32, (2 * blk, LANES), 1) < ATTN_HEAD_DIM
    scale = 1.0 / math.sqrt(ATTN_HEAD_DIM)
    nil = jnp.zeros((2 * blk, LANES), BF16)
    qnil = jnp.zeros((blk, LANES), BF16)
    row4 = lax.broadcasted_iota(jnp.int32, (4 * blk, LANES), 0)
    lane4 = lax.broadcasted_iota(jnp.int32, (4 * blk, LANES), 1)
    ones_cat = jnp.where((row4 < 2 * blk) == (lane4 < ATTN_HEAD_DIM), 1.0, 0.0).astype(BF16)

    def history(i, d, cur_ref, prev_ref, ls):
        if d >= nblk:
            return prev_ref[0, pl.ds(pl.multiple_of(i * blk, blk), blk), ls]
        inside = cur_ref[0, pl.ds(pl.multiple_of(jnp.maximum(i - d, 0) * blk, blk), blk), ls]
        before = prev_ref[0, pl.ds(pl.multiple_of(jnp.minimum(i, d - 1) * blk, blk), blk), ls]
        return jnp.where(i >= d, inside, before)

    def block(i, carry):
        rows = pl.ds(pl.multiple_of(i * blk, blk), blk)
        for gi, d in enumerate(dilations):
            q_ref, k_ref, _, kp_ref, _ = ins[5 * gi:5 * gi + 5]
            has_prev = (i >= d) | (sp > 0)
            neg = jnp.where(has_prev, negs[gi][0], negs[gi][1])
            for pp in range(pairs):
                ls = slice(pp * LANES, (pp + 1) * LANES)
                u = gi * pairs + pp
                qp = q_ref[0, rows, ls] * scale
                q2 = jnp.concatenate([jnp.where(lo_half, qp, qnil), jnp.where(lo_half, qnil, qp)], axis=0)
                kk = jnp.concatenate([history(i, d, k_ref, kp_ref, ls), k_ref[0, rows, ls]], axis=0)
                s2 = _dot_nt(q2, kk)
                h0 = (hb * pairs + pp) * 2
                s_ref[u, 0:blk, :] = s2[0:blk] + slopes_ref[gi, h0] * neg
                s_ref[u, blk:2 * blk, :] = s2[blk:2 * blk] + slopes_ref[gi, h0 + 1] * neg
        ms = []
        for u in range(ng * pairs):
            s2 = s_ref[u]
            m = jnp.max(s2, axis=-1, keepdims=True)
            p = jnp.exp(s2 - m).astype(BF16)
            p_ref[u] = jnp.concatenate([p[0:blk], p[blk:2 * blk]], axis=1)
            ms.append(m)
        for gi, d in enumerate(dilations):
            _, _, v_ref, _, vp_ref = ins[5 * gi:5 * gi + 5]
            if d == 1:
                nat_rows = rows
            else:
                nat_rows = pl.ds((i // d) * (blk * d) + i % d, blk, stride=d)
            for pp in range(pairs):
                ls = slice(pp * LANES, (pp + 1) * LANES)
                u = gi * pairs + pp
                vv = jnp.concatenate([history(i, d, v_ref, vp_ref, ls), v_ref[0, rows, ls]], axis=0)
                vcat = jnp.concatenate([jnp.where(lo_half2, vv, nil), jnp.where(lo_half2, nil, vv)], axis=0)
                od = _dot(p_ref[u], jnp.concatenate([vcat, ones_cat], axis=1))
                den = od[:, LANES:]
                onat_ref[u, nat_rows, :] = od[:, :LANES] * (1.0 / den)
                lnat_ref[u, nat_rows, :] = (jnp.where(lo_half, ms[u][0:blk], ms[u][blk:2 * blk])
                                            + jnp.log(den))
        return carry

    lax.fori_loop(0, nblk, block, 0)

    def combine(c, carry):
        rows = pl.ds(pl.multiple_of(c * blk, blk), blk)
        for pp in range(pairs):
            ls_ = [lnat_ref[gi * pairs + pp, rows, :] for gi in range(ng)]
            m = functools.reduce(jnp.maximum, ls_)
            es = [jnp.exp(l - m) for l in ls_]
            mix = sum(e * onat_ref[gi * pairs + pp, rows, :] for gi, e in enumerate(es)) / sum(es)
            out_ref[0, rows, pp * LANES:(pp + 1) * LANES] = mix.astype(out_ref.dtype)
        return carry

    lax.fori_loop(0, nblk, combine, 0)


def _attn_fused(qkvs, slopes, bsz, s, dilations, pairs):
    w = HEADS_PER_GROUP * ATTN_HEAD_DIM
    cw = pairs * LANES
    ncb = w // cw
    span = ATTN_BLOCK * max(dilations)
    in_specs, args = [pl.BlockSpec(memory_space=pltpu.SMEM)], [slopes]
    for g, d in enumerate(dilations):
        for part in range(3):
            in_specs.append(pl.BlockSpec((1, span, cw),
                                         lambda b, sp, hb, part=part: (b, sp, part * ncb + hb)))
        hist_rows = ATTN_BLOCK * d
        per_span = span // hist_rows
        for part in (1, 2):
            in_specs.append(pl.BlockSpec(
                (1, hist_rows, cw),
                lambda b, sp, hb, part=part, ps=per_span: (b, jnp.maximum(sp * ps - 1, 0), part * ncb + hb)))
        args += [qkvs[g]] * 5
    units = len(dilations) * pairs
    return pl.pallas_call(
        functools.partial(_attn_fused_kernel, dilations=dilations, pairs=pairs),
        grid=(bsz, s // span, ncb),
        in_specs=in_specs,
        out_specs=pl.BlockSpec((1, span, cw), lambda b, sp, hb: (b, sp, hb)),
        out_shape=jax.ShapeDtypeStruct((bsz, s, w), BF16),
        scratch_shapes=[pltpu.VMEM((units, 2 * ATTN_BLOCK, 2 * ATTN_BLOCK), F32),
                        pltpu.VMEM((units, ATTN_BLOCK, 4 * ATTN_BLOCK), BF16),
                        pltpu.VMEM((units, span, LANES), F32),
                        pltpu.VMEM((units, span, LANES), F32)],
        compiler_params=_params("parallel", "parallel", "arbitrary"),
        name="dilated_attn_fused",
    )(*args)


def _alibi_slopes(n_groups):
    n = n_groups * HEADS_PER_GROUP
    i = jnp.arange(1, n + 1, dtype=F32)
    return jnp.exp2(-8.0 * i / n).reshape(n_groups, HEADS_PER_GROUP)


def _mamba_layer(x2, bsz, s, g_mix, w_in, conv_w, conv_b, dt_bias, a_log, d_skip, norm_w, w_out):
    d_model = x2.shape[1]
    d_inner = w_out.shape[0]
    heads = d_inner // SSM_HEAD_DIM
    n_zx = w_in.shape[1] - heads
    assert heads <= LANES and s % CHUNK == 0
    w_in = w_in.astype(BF16)
    w_dt = jnp.pad(w_in[:, n_zx:], ((0, 0), (0, LANES - heads)))
    dtb = jnp.pad(dt_bias, (0, LANES - heads)).reshape(1, LANES)
    alog = jnp.pad(a_log, (0, LANES - heads)).reshape(1, LANES)
    zx, dt = _inproj(x2, g_mix.reshape(1, d_model), w_in, n_zx, w_dt, dtb, tm=1024, tn=2560)

    lane_head = jnp.arange(d_inner, dtype=jnp.int32) // SSM_HEAD_DIM
    expand = (jnp.arange(LANES, dtype=jnp.int32)[:, None] == lane_head[None, :]).astype(BF16)
    expand = jnp.concatenate([expand, expand], axis=0)
    d_skip_x = jnp.repeat(d_skip, SSM_HEAD_DIM).reshape(1, d_inner)
    y = _ssd(zx.reshape(bsz, s, n_zx), dt.reshape(bsz, s, LANES), conv_w,
             conv_b.reshape(1, -1), alog, d_skip_x, norm_w.reshape(1, d_inner), expand, d_inner,
             chunks_per_step=4)
    return _matmul_res(y.reshape(bsz * s, d_inner), w_out.astype(BF16), x2, tm=1024, tn=1024)


def _attn_layer(x2, bsz, s, g_mix, w_qkv, w_o):
    d_model = x2.shape[1]
    n_groups = len(ATTN_GROUPS)
    assert all(win // dil == ATTN_BLOCK and s % win == 0 for win, dil in ATTN_GROUPS)
    dilations = tuple(dil for _, dil in ATTN_GROUPS)
    w = HEADS_PER_GROUP * ATTN_HEAD_DIM
    qkvs = _qkv_proj(x2, g_mix.reshape(1, d_model), w_qkv.astype(BF16), bsz, s, dilations,
                     tm=1024, tn=w)
    mix = _attn_fused(qkvs, _alibi_slopes(n_groups), bsz, s, dilations, pairs=2)
    return _matmul_res(mix.reshape(bsz * s, w), w_o.astype(BF16), x2, tm=1024, tn=2048)


def kernel(x, norm_mix, norm_mlp, ssm_w_in, ssm_conv_w, ssm_conv_b, ssm_dt_bias, ssm_a_log, ssm_d,
           ssm_norm_w, ssm_w_out, attn_w_qkv, attn_w_o, mlp_w1, mlp_w2, final_norm):
    bsz, s, d_model = x.shape
    depth = norm_mix.shape[0]
    x2 = x.reshape(bsz * s, d_model)
    fg = final_norm.reshape(1, d_model)
    w1, w2 = mlp_w1.astype(BF16), mlp_w2.astype(BF16)
    for i in range(depth):
        j = i // 2
        if i % 2 == 0:
            x2 = _mamba_layer(x2, bsz, s, norm_mix[i], ssm_w_in[j], ssm_conv_w[j], ssm_conv_b[j],
                              ssm_dt_bias[j], ssm_a_log[j], ssm_d[j], ssm_norm_w[j], ssm_w_out[j])
        else:
            x2 = _attn_layer(x2, bsz, s, norm_mix[i], attn_w_qkv[j], attn_w_o[j])
        x2 = _mlp(x2, norm_mlp[i].reshape(1, d_model), w1, w2, i, fg,
                  final_norm=(i == depth - 1), tm=1024, tf=1024)
    return x2.reshape(bsz, s, d_model)
```

```python
import functools
import math

import jax
import jax.numpy as jnp
from jax import lax
from jax.experimental import pallas as pl
from jax.experimental.pallas import tpu as pltpu

F32 = jnp.float32
BF16 = jnp.bfloat16

EPS = 1e-5
SSM_HEAD_DIM = 64
SSM_GROUPS = 8
SSM_STATE = 128
CONV_WIDTH = 4
CHUNK = 128
ATTN_GROUPS = ((128, 1), (512, 4), (2048, 16))
HEADS_PER_GROUP = 16
ATTN_HEAD_DIM = 64
ATTN_BLOCK = 128
LANES = 128

VMEM_LIMIT = 60 * 1024 * 1024


def _params(*sem):
    return pltpu.CompilerParams(dimension_semantics=sem, vmem_limit_bytes=VMEM_LIMIT)


def _rms(x, g):
    return x * lax.rsqrt(jnp.mean(x * x, axis=-1, keepdims=True) + EPS) * g


def _silu(x):
    h = 0.5 * x
    return h + h * jnp.tanh(h)


def _softplus(v):
    return jnp.maximum(v, 0.0) + jnp.log1p(jnp.exp(-jnp.abs(v)))


def _split3(v):
    hi = v.astype(BF16)
    r = v - hi.astype(F32)
    mid = r.astype(BF16)
    lo = (r - mid.astype(F32)).astype(BF16)
    return hi, mid, lo


def _dot(a, b):
    return jnp.dot(a, b, preferred_element_type=F32)


def _dot_nt(a, b):
    return lax.dot_general(a, b, (((1,), (1,)), ((), ())), preferred_element_type=F32)


def _dot_tn(a, b):
    return lax.dot_general(a, b, (((0,), (0,)), ((), ())), preferred_element_type=F32)


def _hi_lo(v):
    hi = v.astype(BF16)
    lo = (v - hi.astype(F32)).astype(BF16)
    return jnp.concatenate([hi, lo], axis=1)


def _perm_view_shape(bsz, s, dilation, tm, c):
    span = ATTN_BLOCK * dilation
    return (bsz, s // span, dilation, span // tm, tm // dilation, c)


def _perm_block_spec(s, dilation, tm, cblk, col_of):
    span = ATTN_BLOCK * dilation
    per_seq, per_span = s // tm, span // tm

    def index(i, *rest):
        ib = i % per_seq
        return (i // per_seq, ib // per_span, 0, ib % per_span, 0, col_of(*rest))

    return pl.BlockSpec((None, None, dilation, None, tm // dilation, cblk), index)


def _perm_pieces(tm, dilation):
    span = ATTN_BLOCK * dilation
    if tm >= span:
        return [(w0 + r * ATTN_BLOCK, w0 + r, ATTN_BLOCK)
                for w0 in range(0, tm, span) for r in range(dilation)]
    p = tm // dilation
    return [(r * p, r, p) for r in range(dilation)]


def _qkv_proj_kernel(x_ref, g_ref, w_ref, *refs, dilations):
    ng = len(dilations)
    o_refs, hf_ref, hp_ref = refs[:ng], refs[ng], refs[ng + 1]
    j = pl.program_id(1)
    tm, k = x_ref.shape

    @pl.when(j == 0)
    def _():
        x = x_ref[...]
        inv = lax.rsqrt(jnp.mean(x * x, axis=-1, keepdims=True) + EPS)
        for c in range(k // LANES):
            cs = slice(c * LANES, (c + 1) * LANES)
            slot = c % hf_ref.shape[0]
            h = x_ref[:, cs] * inv * g_ref[:, cs]
            hf_ref[slot] = h
            for gi, d in enumerate(dilations):
                if d == 1:
                    hp_ref[gi, :, cs] = h.astype(BF16)
                    continue
                for dst, src, p in _perm_pieces(tm, d):
                    hp_ref[gi, dst:dst + p, cs] = hf_ref[slot, pl.ds(src, p, stride=d), :].astype(BF16)

    res = _dot(hp_ref[j // 3], w_ref[...]).astype(BF16)
    for gi, d in enumerate(dilations):
        @pl.when(j // 3 == gi)
        def _(gi=gi, d=d):
            if len(o_refs[gi].shape) == 2:
                o_refs[gi][...] = res
            else:
                p = tm // d
                for r in range(d):
                    o_refs[gi][r] = res[r * p:(r + 1) * p, :]


def _qkv_proj(x, g, w, bsz, s, dilations, tm, tn):
    m, k = x.shape
    ng = len(dilations)
    wq = w.shape[1] // ng
    assert wq == 3 * tn and s % tm == 0
    out_specs, out_shapes = [], []
    for gi, d in enumerate(dilations):
        col_of = lambda j, gi=gi: jnp.clip(j - 3 * gi, 0, 2)
        if tm >= ATTN_BLOCK * d:
            out_specs.append(pl.BlockSpec((tm, tn), lambda i, j, c=col_of: (i, c(j))))
            out_shapes.append(jax.ShapeDtypeStruct((m, wq), BF16))
        else:
            out_specs.append(_perm_block_spec(s, d, tm, tn, col_of))
            out_shapes.append(jax.ShapeDtypeStruct(_perm_view_shape(bsz, s, d, tm, wq), BF16))
    outs = pl.pallas_call(
        functools.partial(_qkv_proj_kernel, dilations=dilations),
        grid=(m // tm, 3 * ng),
        in_specs=[pl.BlockSpec((tm, k), lambda i, j: (i, 0)),
                  pl.BlockSpec((1, k), lambda i, j: (0, 0)),
                  pl.BlockSpec((k, tn), lambda i, j: (0, (j % 3) * ng + j // 3))],
        out_specs=out_specs,
        out_shape=out_shapes,
        scratch_shapes=[pltpu.VMEM((2, tm, LANES), F32), pltpu.VMEM((ng, tm, k), BF16)],
        compiler_params=_params("parallel", "arbitrary"),
        name="qkv_proj",
    )(x, g, w)
    return [o.reshape(bsz, s, wq) for o in outs]


def _inproj_kernel(x_ref, g_ref, w_ref, wdt_ref, dtb_ref, zx_ref, dt_ref, h_ref):
    @pl.when(pl.program_id(1) == 0)
    def _():
        h = _rms(x_ref[...], g_ref[...]).astype(BF16)
        h_ref[...] = h
        dt_ref[...] = _softplus(_dot(h, wdt_ref[...]) + dtb_ref[...])

    zx_ref[...] = _dot(h_ref[...], w_ref[...]).astype(zx_ref.dtype)


def _inproj(x, g, w_in, n, w_dt, dt_bias, tm, tn):
    m, k = x.shape
    assert n % tn == 0
    return pl.pallas_call(
        _inproj_kernel,
        grid=(m // tm, n // tn),
        in_specs=[pl.BlockSpec((tm, k), lambda i, j: (i, 0)),
                  pl.BlockSpec((1, k), lambda i, j: (0, 0)),
                  pl.BlockSpec((k, tn), lambda i, j: (0, j)),
                  pl.BlockSpec((k, LANES), lambda i, j: (0, 0)),
                  pl.BlockSpec((1, LANES), lambda i, j: (0, 0))],
        out_specs=[pl.BlockSpec((tm, tn), lambda i, j: (i, j)),
                   pl.BlockSpec((tm, LANES), lambda i, j: (i, 0))],
        out_shape=[jax.ShapeDtypeStruct((m, n), BF16),
                   jax.ShapeDtypeStruct((m, LANES), F32)],
        scratch_shapes=[pltpu.VMEM((tm, k), BF16)],
        compiler_params=_params("parallel", "arbitrary"),
        name="ssm_in_proj",
    )(x, g, w_in, w_dt, dt_bias)


def _matmul_res_kernel(a_ref, w_ref, r_ref, o_ref):
    o_ref[...] = r_ref[...] + _dot(a_ref[...], w_ref[...])


def _matmul_res(a, w, res, tm, tn):
    m, k = a.shape
    n = w.shape[1]
    return pl.pallas_call(
        _matmul_res_kernel,
        grid=(m // tm, n // tn),
        in_specs=[pl.BlockSpec((tm, k), lambda i, j: (i, 0)),
                  pl.BlockSpec((k, tn), lambda i, j: (0, j)),
                  pl.BlockSpec((tm, tn), lambda i, j: (i, j))],
        out_specs=pl.BlockSpec((tm, tn), lambda i, j: (i, j)),
        out_shape=jax.ShapeDtypeStruct((m, n), F32),
        compiler_params=_params("parallel", "arbitrary"),
        name="matmul_residual",
    )(a, w, res)


def _mlp_kernel(x_ref, g_ref, w1_ref, w2_ref, fg_ref, o_ref, h_ref, *, final_norm):
    f = pl.program_id(1)

    @pl.when(f == 0)
    def _():
        x = x_ref[...]
        h_ref[...] = _rms(x, g_ref[...]).astype(BF16)
        o_ref[...] = x

    a = jnp.maximum(_dot(h_ref[...], w1_ref[...]), 0.0)
    o_ref[...] += _dot((a * a).astype(BF16), w2_ref[...])

    if final_norm:
        @pl.when(f == pl.num_programs(1) - 1)
        def _():
            o_ref[...] = _rms(o_ref[...], fg_ref[...])


def _mlp(x, g, w1, w2, layer, fg, final_norm, tm, tf):
    m, d = x.shape
    ff = w1.shape[2]
    return pl.pallas_call(
        functools.partial(_mlp_kernel, final_norm=final_norm),
        grid=(m // tm, ff // tf),
        in_specs=[pl.BlockSpec((tm, d), lambda i, f: (i, 0)),
                  pl.BlockSpec((1, d), lambda i, f: (0, 0)),
                  pl.BlockSpec((None, d, tf), lambda i, f: (layer, 0, f)),
                  pl.BlockSpec((None, tf, d), lambda i, f: (layer, f, 0)),
                  pl.BlockSpec((1, d), lambda i, f: (0, 0))],
        out_specs=pl.BlockSpec((tm, d), lambda i, f: (i, 0)),
        out_shape=jax.ShapeDtypeStruct((m, d), F32),
        scratch_shapes=[pltpu.VMEM((tm, d), BF16)],
        compiler_params=_params("parallel", "arbitrary"),
        name="mlp",
    )(x, g, w1, w2, fg)


def _ssd_kernel(zx_ref, dt_ref, cw_ref, cb_ref, alog_ref, dsk_ref, nw_ref, e_ref, shift_ref, y_ref,
                ucat_ref, xc_ref, state_ref, *, d_inner):
    @pl.when(pl.program_id(1) == 0)
    def _():
        ucat_ref[0:CHUNK, :] = jnp.zeros((CHUNK, ucat_ref.shape[1]), BF16)
        state_ref[...] = jnp.zeros_like(state_ref)

    def chunk(ci, carry):
        rows = pl.ds(pl.multiple_of(ci * CHUNK, CHUNK), CHUNK)
        _ssd_chunk(zx_ref.at[0, rows], dt_ref.at[0, rows], cw_ref, cb_ref, alog_ref, dsk_ref, nw_ref,
                   e_ref, shift_ref, y_ref.at[0, rows], ucat_ref, xc_ref, state_ref, d_inner=d_inner)
        return carry

    lax.fori_loop(0, zx_ref.shape[1] // CHUNK, chunk, 0)


def _ssd_chunk(zx_ref, dt_ref, cw_ref, cb_ref, alog_ref, dsk_ref, nw_ref, e_ref, shift_ref, y_ref,
               ucat_ref, xc_ref, state_ref, *, d_inner):
    q = CHUNK
    conv_dim = d_inner + 2 * SSM_GROUPS * SSM_STATE
    gw = d_inner // SSM_GROUPS
    hpg = gw // SSM_HEAD_DIM

    cpiece = 512
    for c0 in range(0, conv_dim, cpiece):
        cs = slice(c0, c0 + cpiece)
        u = zx_ref[:, d_inner + c0:d_inner + c0 + cpiece]
        ucat_ref[q:2 * q, cs] = u
        back = _dot(shift_ref[...], ucat_ref[:, cs])
        acc = cb_ref[:, cs] + cw_ref[CONV_WIDTH - 1:CONV_WIDTH, cs] * u.astype(F32)
        for k in range(1, CONV_WIDTH):
            w_k = cw_ref[CONV_WIDTH - 1 - k:CONV_WIDTH - k, cs]
            acc = acc + w_k * back[(k - 1) * q:k * q]
        xc_ref[:, cs] = _silu(acc)
        ucat_ref[0:q, cs] = u

    dt = dt_ref[...]
    a = dt * (-jnp.exp(alog_ref[...]))
    row = lax.broadcasted_iota(jnp.int32, (q, q), 0)
    col = lax.broadcasted_iota(jnp.int32, (q, q), 1)
    causal = row >= col
    tril = causal.astype(BF16)
    a_hi, a_mid, a_lo = _split3(a)
    a_cs = _dot(tril, a_hi) + _dot(tril, a_mid) + _dot(tril, a_lo)
    a_cs_t = a_cs.T
    a_last = a_cs[q - 1:q, :]

    per_head = jnp.concatenate([_hi_lo(dt), _hi_lo(jnp.exp(a_cs)), _hi_lo(jnp.exp(a_last - a_cs))],
                               axis=0)

    lane = lax.broadcasted_iota(jnp.int32, (q, LANES), 1)
    lo_half = lane < SSM_HEAD_DIM

    for g in range(SSM_GROUPS):
        gs = slice(g * gw, (g + 1) * gw)
        bg = xc_ref[:, d_inner + g * SSM_STATE:d_inner + (g + 1) * SSM_STATE].astype(BF16)
        cg = xc_ref[:, d_inner + (SSM_GROUPS + g) * SSM_STATE:
                    d_inner + (SSM_GROUPS + g + 1) * SSM_STATE].astype(BF16)
        cb = _dot_nt(cg, bg)
        xs = xc_ref[:, gs]
        lanes_x = _dot(per_head, e_ref[:, gs])
        dt_x, grow_x, dec_x = lanes_x[0:q], lanes_x[q:2 * q], lanes_x[2 * q:3 * q]
        xdt = xs * dt_x
        st = state_ref[:, gs]
        y = _dot(cg, st.astype(BF16)) * grow_x
        new_st = _dot_tn(bg, (xdt * dec_x).astype(BF16))
        state_ref[:, gs] = st * grow_x[q - 1:q] + new_st

        ydiag = []
        for pr in range(hpg // 2):
            lhs = []
            for half in range(2):
                h = g * hpg + 2 * pr + half
                diff = a_cs[:, h:h + 1] - a_cs_t[h:h + 1, :]
                decay = jnp.exp(jnp.where(causal, diff, -jnp.inf))
                lhs.append((cb * decay).astype(BF16))
            xp = xdt[:, pr * LANES:(pr + 1) * LANES]
            rhs = jnp.concatenate([jnp.where(lo_half, xp, 0.0), jnp.where(lo_half, 0.0, xp)],
                                  axis=0).astype(BF16)
            ydiag.append(_dot(jnp.concatenate(lhs, axis=1), rhs))
        y = y + jnp.concatenate(ydiag, axis=1) + dsk_ref[:, gs] * xs

        gated = y * _silu(zx_ref[:, gs].astype(F32))
        yn = gated * lax.rsqrt(jnp.mean(gated * gated, axis=-1, keepdims=True) + EPS)
        y_ref[:, gs] = (yn * nw_ref[:, gs]).astype(y_ref.dtype)


def _ssd(zx, dt, conv_w, conv_b, a_log, d_skip_x, norm_w, expand, d_inner, chunks_per_step):
    bsz, s, n_zx = zx.shape
    conv_dim = n_zx - d_inner
    rows_per_step = chunks_per_step * CHUNK
    assert s % rows_per_step == 0
    kern = functools.partial(_ssd_kernel, d_inner=d_inner)
    const = lambda b, c: (0, 0)
    rows = jnp.arange((CONV_WIDTH - 1) * CHUNK, dtype=jnp.int32)
    src = CHUNK + rows % CHUNK - (rows // CHUNK + 1)
    shift = (jnp.arange(2 * CHUNK, dtype=jnp.int32)[None, :] == src[:, None]).astype(BF16)
    return pl.pallas_call(
        kern,
        grid=(bsz, s // rows_per_step),
        in_specs=[pl.BlockSpec((1, rows_per_step, n_zx), lambda b, c: (b, c, 0)),
                  pl.BlockSpec((1, rows_per_step, LANES), lambda b, c: (b, c, 0)),
                  pl.BlockSpec((CONV_WIDTH, conv_dim), const),
                  pl.BlockSpec((1, conv_dim), const),
                  pl.BlockSpec((1, LANES), const),
                  pl.BlockSpec((1, d_inner), const),
                  pl.BlockSpec((1, d_inner), const),
                  pl.BlockSpec((2 * LANES, d_inner), const),
                  pl.BlockSpec(((CONV_WIDTH - 1) * CHUNK, 2 * CHUNK), const)],
        out_specs=pl.BlockSpec((1, rows_per_step, d_inner), lambda b, c: (b, c, 0)),
        out_shape=jax.ShapeDtypeStruct((bsz, s, d_inner), BF16),
        scratch_shapes=[pltpu.VMEM((2 * CHUNK, conv_dim), BF16),
                        pltpu.VMEM((CHUNK, conv_dim), F32),
                        pltpu.VMEM((SSM_STATE, d_inner), F32)],
        compiler_params=_params("parallel", "arbitrary"),
        name="ssd_chunk",
    )(zx, dt, conv_w, conv_b, a_log, d_skip_x, norm_w, expand, shift)


def _attn_fused_kernel(slopes_ref, *refs, dilations, pairs):
    ng = len(dilations)
    ins, out_ref = refs[:5 * ng], refs[5 * ng]
    s_ref, p_ref, onat_ref, lnat_ref = refs[5 * ng + 1:]
    blk = ATTN_BLOCK
    sp, hb = pl.program_id(1), pl.program_id(2)
    nblk = out_ref.shape[1] // blk

    qi = lax.broadcasted_iota(jnp.int32, (blk, 2 * blk), 0)
    kj = lax.broadcasted_iota(jnp.int32, (blk, 2 * blk), 1)
    dist = qi + blk - kj
    band = (dist >= 0) & (dist <= blk)
    cur_only = band & (kj >= blk)
    negs = []
    for d in dilations:
        nd = -(dist * d).astype(F32)
        negs.append((jnp.where(band, nd, -jnp.inf), jnp.where(cur_only, nd, -jnp.inf)))
    lo_half = lax.broadcasted_iota(jnp.int32, (blk, LANES), 1) < ATTN_HEAD_DIM
    lo_half2 = lax.broadcasted_iota(jnp.int32, (2 * blk, LANES), 1) < ATTN_HEAD_DIM
    scale = 1.0 / math.sqrt(ATTN_HEAD_DIM)
    nil = jnp.zeros((2 * blk, LANES), BF16)
    qnil = jnp.zeros((blk, LANES), BF16)
    row4 = lax.broadcasted_iota(jnp.int32, (4 * blk, LANES), 0)
    lane4 = lax.broadcasted_iota(jnp.int32, (4 * blk, LANES), 1)
    ones_cat = jnp.where((row4 < 2 * blk) == (lane4 < ATTN_HEAD_DIM), 1.0, 0.0).astype(BF16)

    def history(i, d, cur_ref, prev_ref, ls):
        if d >= nblk:
            return prev_ref[0, pl.ds(pl.multiple_of(i * blk, blk), blk), ls]
        inside = cur_ref[0, pl.ds(pl.multiple_of(jnp.maximum(i - d, 0) * blk, blk), blk), ls]
        before = prev_ref[0, pl.ds(pl.multiple_of(jnp.minimum(i, d - 1) * blk, blk), blk), ls]
        return jnp.where(i >= d, inside, before)

    def block(i, carry):
        rows = pl.ds(pl.multiple_of(i * blk, blk), blk)
        for gi, d in enumerate(dilations):
            q_ref, k_ref, _, kp_ref, _ = ins[5 * gi:5 * gi + 5]
            has_prev = (i >= d) | (sp > 0)
            neg = jnp.where(has_prev, negs[gi][0], negs[gi][1])
            for pp in range(pairs):
                ls = slice(pp * LANES, (pp + 1) * LANES)
                u = gi * pairs + pp
                qp = q_ref[0, rows, ls] * scale
                q2 = jnp.concatenate([jnp.where(lo_half, qp, qnil), jnp.where(lo_half, qnil, qp)], axis=0)
                kk = jnp.concatenate([history(i, d, k_ref, kp_ref, ls), k_ref[0, rows, ls]], axis=0)
                s2 = _dot_nt(q2, kk)
                h0 = (hb * pairs + pp) * 2
                s_ref[u, 0:blk, :] = s2[0:blk] + slopes_ref[gi, h0] * neg
                s_ref[u, blk:2 * blk, :] = s2[blk:2 * blk] + slopes_ref[gi, h0 + 1] * neg
        ms = []
        for u in range(ng * pairs):
            s2 = s_ref[u]
            m = jnp.max(s2, axis=-1, keepdims=True)
            p = jnp.exp(s2 - m).astype(BF16)
            p_ref[u] = jnp.concatenate([p[0:blk], p[blk:2 * blk]], axis=1)
            ms.append(m)
        for gi, d in enumerate(dilations):
            _, _, v_ref, _, vp_ref = ins[5 * gi:5 * gi + 5]
            if d == 1:
                nat_rows = rows
            else:
                nat_rows = pl.ds((i // d) * (blk * d) + i % d, blk, stride=d)
            for pp in range(pairs):
                ls = slice(pp * LANES, (pp + 1) * LANES)
                u = gi * pairs + pp
                vv = jnp.concatenate([history(i, d, v_ref, vp_ref, ls), v_ref[0, rows, ls]], axis=0)
                vcat = jnp.concatenate([jnp.where(lo_half2, vv, nil), jnp.where(lo_half2, nil, vv)], axis=0)
                od = _dot(p_ref[u], jnp.concatenate([vcat, ones_cat], axis=1))
                den = od[:, LANES:]
                onat_ref[u, nat_rows, :] = od[:, :LANES] * (1.0 / den)
                lnat_ref[u, nat_rows, :] = (jnp.where(lo_half, ms[u][0:blk], ms[u][blk:2 * blk])
                                            + jnp.log(den))
        return carry

    lax.fori_loop(0, nblk, block, 0, unroll=2)

    def combine(c, carry):
        rows = pl.ds(pl.multiple_of(c * blk, blk), blk)
        for pp in range(pairs):
            ls_ = [lnat_ref[gi * pairs + pp, rows, :] for gi in range(ng)]
            m = functools.reduce(jnp.maximum, ls_)
            es = [jnp.exp(l - m) for l in ls_]
            mix = sum(e * onat_ref[gi * pairs + pp, rows, :] for gi, e in enumerate(es)) / sum(es)
            out_ref[0, rows, pp * LANES:(pp + 1) * LANES] = mix.astype(out_ref.dtype)
        return carry

    lax.fori_loop(0, nblk, combine, 0)


def _attn_fused(qkvs, slopes, bsz, s, dilations, pairs):
    w = HEADS_PER_GROUP * ATTN_HEAD_DIM
    cw = pairs * LANES
    ncb = w // cw
    span = ATTN_BLOCK * max(dilations)
    in_specs, args = [pl.BlockSpec(memory_space=pltpu.SMEM)], [slopes]
    for g, d in enumerate(dilations):
        for part in range(3):
            in_specs.append(pl.BlockSpec((1, span, cw),
                                         lambda b, sp, hb, part=part: (b, sp, part * ncb + hb)))
        hist_rows = ATTN_BLOCK * d
        per_span = span // hist_rows
        for part in (1, 2):
            in_specs.append(pl.BlockSpec(
                (1, hist_rows, cw),
                lambda b, sp, hb, part=part, ps=per_span: (b, jnp.maximum(sp * ps - 1, 0), part * ncb + hb)))
        args += [qkvs[g]] * 5
    units = len(dilations) * pairs
    return pl.pallas_call(
        functools.partial(_attn_fused_kernel, dilations=dilations, pairs=pairs),
        grid=(bsz, s // span, ncb),
        in_specs=in_specs,
        out_specs=pl.BlockSpec((1, span, cw), lambda b, sp, hb: (b, sp, hb)),
        out_shape=jax.ShapeDtypeStruct((bsz, s, w), BF16),
        scratch_shapes=[pltpu.VMEM((units, 2 * ATTN_BLOCK, 2 * ATTN_BLOCK), F32),
                        pltpu.VMEM((units, ATTN_BLOCK, 4 * ATTN_BLOCK), BF16),
                        pltpu.VMEM((units, span, LANES), F32),
                        pltpu.VMEM((units, span, LANES), F32)],
        compiler_params=_params("parallel", "parallel", "arbitrary"),
        name="dilated_attn_fused",
    )(*args)


def _alibi_slopes(n_groups):
    n = n_groups * HEADS_PER_GROUP
    i = jnp.arange(1, n + 1, dtype=F32)
    return jnp.exp2(-8.0 * i / n).reshape(n_groups, HEADS_PER_GROUP)


def _mamba_layer(x2, bsz, s, g_mix, w_in, conv_w, conv_b, dt_bias, a_log, d_skip, norm_w, w_out):
    d_model = x2.shape[1]
    d_inner = w_out.shape[0]
    heads = d_inner // SSM_HEAD_DIM
    n_zx = w_in.shape[1] - heads
    assert heads <= LANES and s % CHUNK == 0
    w_in = w_in.astype(BF16)
    w_dt = jnp.pad(w_in[:, n_zx:], ((0, 0), (0, LANES - heads)))
    dtb = jnp.pad(dt_bias, (0, LANES - heads)).reshape(1, LANES)
    alog = jnp.pad(a_log, (0, LANES - heads)).reshape(1, LANES)
    zx, dt = _inproj(x2, g_mix.reshape(1, d_model), w_in, n_zx, w_dt, dtb, tm=1024, tn=2560)

    lane_head = jnp.arange(d_inner, dtype=jnp.int32) // SSM_HEAD_DIM
    expand = (jnp.arange(LANES, dtype=jnp.int32)[:, None] == lane_head[None, :]).astype(BF16)
    expand = jnp.concatenate([expand, expand], axis=0)
    d_skip_x = jnp.repeat(d_skip, SSM_HEAD_DIM).reshape(1, d_inner)
    y = _ssd(zx.reshape(bsz, s, n_zx), dt.reshape(bsz, s, LANES), conv_w,
             conv_b.reshape(1, -1), alog, d_skip_x, norm_w.reshape(1, d_inner), expand, d_inner,
             chunks_per_step=4)
    return _matmul_res(y.reshape(bsz * s, d_inner), w_out.astype(BF16), x2, tm=1024, tn=1024)


def _attn_layer(x2, bsz, s, g_mix, w_qkv, w_o):
    d_model = x2.shape[1]
    n_groups = len(ATTN_GROUPS)
    assert all(win // dil == ATTN_BLOCK and s % win == 0 for win, dil in ATTN_GROUPS)
    dilations = tuple(dil for _, dil in ATTN_GROUPS)
    w = HEADS_PER_GROUP * ATTN_HEAD_DIM
    qkvs = _qkv_proj(x2, g_mix.reshape(1, d_model), w_qkv.astype(BF16), bsz, s, dilations,
                     tm=1024, tn=w)
    mix = _attn_fused(qkvs, _alibi_slopes(n_groups), bsz, s, dilations, pairs=2)
    return _matmul_res(mix.reshape(bsz * s, w), w_o.astype(BF16), x2, tm=1024, tn=2048)


def kernel(x, norm_mix, norm_mlp, ssm_w_in, ssm_conv_w, ssm_conv_b, ssm_dt_bias, ssm_a_log, ssm_d,
           ssm_norm_w, ssm_w_out, attn_w_qkv, attn_w_o, mlp_w1, mlp_w2, final_norm):
    bsz, s, d_model = x.shape
    depth = norm_mix.shape[0]
    x2 = x.reshape(bsz * s, d_model)
    fg = final_norm.reshape(1, d_model)
    w1, w2 = mlp_w1.astype(BF16), mlp_w2.astype(BF16)
    for i in range(depth):
        j = i // 2
        if i % 2 == 0:
            x2 = _mamba_layer(x2, bsz, s, norm_mix[i], ssm_w_in[j], ssm_conv_w[j], ssm_conv_b[j],
                              ssm_dt_bias[j], ssm_a_log[j], ssm_d[j], ssm_norm_w[j], ssm_w_out[j])
        else:
            x2 = _attn_layer(x2, bsz, s, norm_mix[i], attn_w_qkv[j], attn_w_o[j])
        x2 = _mlp(x2, norm_mlp[i].reshape(1, d_model), w1, w2, i, fg,
                  final_norm=(i == depth - 1), tm=1024, tf=1024)
    return x2.reshape(bsz, s, d_model)
```

```python
import functools
import math

import jax
import jax.numpy as jnp
from jax import lax
from jax.experimental import pallas as pl
from jax.experimental.pallas import tpu as pltpu

F32 = jnp.float32
BF16 = jnp.bfloat16

EPS = 1e-5
SSM_HEAD_DIM = 64
SSM_GROUPS = 8
SSM_STATE = 128
CONV_WIDTH = 4
CHUNK = 128
ATTN_GROUPS = ((128, 1), (512, 4), (2048, 16))
HEADS_PER_GROUP = 16
ATTN_HEAD_DIM = 64
ATTN_BLOCK = 128
LANES = 128

VMEM_LIMIT = 60 * 1024 * 1024


def _params(*sem):
    return pltpu.CompilerParams(dimension_semantics=sem, vmem_limit_bytes=VMEM_LIMIT)


def _rms(x, g):
    return x * lax.rsqrt(jnp.mean(x * x, axis=-1, keepdims=True) + EPS) * g


def _silu(x):
    h = 0.5 * x
    return h + h * jnp.tanh(h)


def _softplus(v):
    return jnp.maximum(v, 0.0) + jnp.log1p(jnp.exp(-jnp.abs(v)))


def _split3(v):
    hi = v.astype(BF16)
    r = v - hi.astype(F32)
    mid = r.astype(BF16)
    lo = (r - mid.astype(F32)).astype(BF16)
    return hi, mid, lo


def _dot(a, b):
    return jnp.dot(a, b, preferred_element_type=F32)


def _dot_nt(a, b):
    return lax.dot_general(a, b, (((1,), (1,)), ((), ())), preferred_element_type=F32)


def _dot_tn(a, b):
    return lax.dot_general(a, b, (((0,), (0,)), ((), ())), preferred_element_type=F32)


def _hi_lo(v):
    hi = v.astype(BF16)
    lo = (v - hi.astype(F32)).astype(BF16)
    return jnp.concatenate([hi, lo], axis=1)


def _perm_view_shape(bsz, s, dilation, tm, c):
    span = ATTN_BLOCK * dilation
    return (bsz, s // span, dilation, span // tm, tm // dilation, c)


def _perm_block_spec(s, dilation, tm, cblk, col_of):
    span = ATTN_BLOCK * dilation
    per_seq, per_span = s // tm, span // tm

    def index(i, *rest):
        ib = i % per_seq
        return (i // per_seq, ib // per_span, 0, ib % per_span, 0, col_of(*rest))

    return pl.BlockSpec((None, None, dilation, None, tm // dilation, cblk), index)


def _perm_pieces(tm, dilation):
    span = ATTN_BLOCK * dilation
    if tm >= span:
        return [(w0 + r * ATTN_BLOCK, w0 + r, ATTN_BLOCK)
                for w0 in range(0, tm, span) for r in range(dilation)]
    p = tm // dilation
    return [(r * p, r, p) for r in range(dilation)]


def _qkv_proj_kernel(x_ref, g_ref, w_ref, *refs, dilations):
    ng = len(dilations)
    o_refs, hf_ref, mid_ref, hp_ref = refs[:ng], refs[ng], refs[ng + 1], refs[ng + 2]
    j = pl.program_id(1)
    tm, k = x_ref.shape

    @pl.when(j == 0)
    def _():
        x = x_ref[...]
        inv = lax.rsqrt(jnp.mean(x * x, axis=-1, keepdims=True) + EPS)
        for c in range(k // LANES):
            cs = slice(c * LANES, (c + 1) * LANES)
            slot = c % hf_ref.shape[0]
            h = x_ref[:, cs] * inv * g_ref[:, cs]
            hf_ref[slot] = h
            for gi, d in enumerate(dilations):
                if d == 1:
                    hp_ref[gi, :, cs] = h.astype(BF16)
                elif d % 16 == 0 and tm < ATTN_BLOCK * d:
                    n, s2, p = tm // 4, d // 4, tm // d
                    for r in range(4):
                        mid_ref[slot, r * n:(r + 1) * n, :] = hf_ref[slot, pl.ds(r, n, stride=4), :]
                    for r in range(4):
                        for r2 in range(s2):
                            dst = (r + 4 * r2) * p
                            hp_ref[gi, dst:dst + p, cs] = (
                                mid_ref[slot, pl.ds(r * n + r2, p, stride=s2), :].astype(BF16))
                else:
                    for dst, src, p in _perm_pieces(tm, d):
                        hp_ref[gi, dst:dst + p, cs] = hf_ref[slot, pl.ds(src, p, stride=d), :].astype(BF16)

    res = _dot(hp_ref[j // 3], w_ref[...]).astype(BF16)
    for gi, d in enumerate(dilations):
        @pl.when(j // 3 == gi)
        def _(gi=gi, d=d):
            if len(o_refs[gi].shape) == 2:
                o_refs[gi][...] = res
            else:
                p = tm // d
                for r in range(d):
                    o_refs[gi][r] = res[r * p:(r + 1) * p, :]


def _qkv_proj(x, g, w, bsz, s, dilations, tm, tn):
    m, k = x.shape
    ng = len(dilations)
    wq = w.shape[1] // ng
    assert wq == 3 * tn and s % tm == 0
    out_specs, out_shapes = [], []
    for gi, d in enumerate(dilations):
        col_of = lambda j, gi=gi: jnp.clip(j - 3 * gi, 0, 2)
        if tm >= ATTN_BLOCK * d:
            out_specs.append(pl.BlockSpec((tm, tn), lambda i, j, c=col_of: (i, c(j))))
            out_shapes.append(jax.ShapeDtypeStruct((m, wq), BF16))
        else:
            out_specs.append(_perm_block_spec(s, d, tm, tn, col_of))
            out_shapes.append(jax.ShapeDtypeStruct(_perm_view_shape(bsz, s, d, tm, wq), BF16))
    outs = pl.pallas_call(
        functools.partial(_qkv_proj_kernel, dilations=dilations),
        grid=(m // tm, 3 * ng),
        in_specs=[pl.BlockSpec((tm, k), lambda i, j: (i, 0)),
                  pl.BlockSpec((1, k), lambda i, j: (0, 0)),
                  pl.BlockSpec((k, tn), lambda i, j: (0, (j % 3) * ng + j // 3))],
        out_specs=out_specs,
        out_shape=out_shapes,
        scratch_shapes=[pltpu.VMEM((2, tm, LANES), F32), pltpu.VMEM((2, tm, LANES), F32),
                        pltpu.VMEM((ng, tm, k), BF16)],
        compiler_params=_params("parallel", "arbitrary"),
        name="qkv_proj",
    )(x, g, w)
    return [o.reshape(bsz, s, wq) for o in outs]


def _inproj_kernel(x_ref, g_ref, w_ref, wdt_ref, dtb_ref, zx_ref, dt_ref, h_ref):
    @pl.when(pl.program_id(1) == 0)
    def _():
        h = _rms(x_ref[...], g_ref[...]).astype(BF16)
        h_ref[...] = h
        dt_ref[...] = _softplus(_dot(h, wdt_ref[...]) + dtb_ref[...])

    zx_ref[...] = _dot(h_ref[...], w_ref[...]).astype(zx_ref.dtype)


def _inproj(x, g, w_in, n, w_dt, dt_bias, tm, tn):
    m, k = x.shape
    assert n % tn == 0
    return pl.pallas_call(
        _inproj_kernel,
        grid=(m // tm, n // tn),
        in_specs=[pl.BlockSpec((tm, k), lambda i, j: (i, 0)),
                  pl.BlockSpec((1, k), lambda i, j: (0, 0)),
                  pl.BlockSpec((k, tn), lambda i, j: (0, j)),
                  pl.BlockSpec((k, LANES), lambda i, j: (0, 0)),
                  pl.BlockSpec((1, LANES), lambda i, j: (0, 0))],
        out_specs=[pl.BlockSpec((tm, tn), lambda i, j: (i, j)),
                   pl.BlockSpec((tm, LANES), lambda i, j: (i, 0))],
        out_shape=[jax.ShapeDtypeStruct((m, n), BF16),
                   jax.ShapeDtypeStruct((m, LANES), F32)],
        scratch_shapes=[pltpu.VMEM((tm, k), BF16)],
        compiler_params=_params("parallel", "arbitrary"),
        name="ssm_in_proj",
    )(x, g, w_in, w_dt, dt_bias)


def _matmul_res_kernel(a_ref, w_ref, r_ref, o_ref):
    o_ref[...] = r_ref[...] + _dot(a_ref[...], w_ref[...])


def _matmul_res(a, w, res, tm, tn):
    m, k = a.shape
    n = w.shape[1]
    return pl.pallas_call(
        _matmul_res_kernel,
        grid=(m // tm, n // tn),
        in_specs=[pl.BlockSpec((tm, k), lambda i, j: (i, 0)),
                  pl.BlockSpec((k, tn), lambda i, j: (0, j)),
                  pl.BlockSpec((tm, tn), lambda i, j: (i, j))],
        out_specs=pl.BlockSpec((tm, tn), lambda i, j: (i, j)),
        out_shape=jax.ShapeDtypeStruct((m, n), F32),
        compiler_params=_params("parallel", "arbitrary"),
        name="matmul_residual",
    )(a, w, res)


def _mlp_kernel(x_ref, g_ref, w1_ref, w2_ref, fg_ref, o_ref, h_ref, *, final_norm):
    f = pl.program_id(1)

    @pl.when(f == 0)
    def _():
        x = x_ref[...]
        h_ref[...] = _rms(x, g_ref[...]).astype(BF16)
        o_ref[...] = x

    a = jnp.maximum(_dot(h_ref[...], w1_ref[...]), 0.0)
    o_ref[...] += _dot((a * a).astype(BF16), w2_ref[...])

    if final_norm:
        @pl.when(f == pl.num_programs(1) - 1)
        def _():
            o_ref[...] = _rms(o_ref[...], fg_ref[...])


def _mlp(x, g, w1, w2, layer, fg, final_norm, tm, tf):
    m, d = x.shape
    ff = w1.shape[2]
    return pl.pallas_call(
        functools.partial(_mlp_kernel, final_norm=final_norm),
        grid=(m // tm, ff // tf),
        in_specs=[pl.BlockSpec((tm, d), lambda i, f: (i, 0)),
                  pl.BlockSpec((1, d), lambda i, f: (0, 0)),
                  pl.BlockSpec((None, d, tf), lambda i, f: (layer, 0, f)),
                  pl.BlockSpec((None, tf, d), lambda i, f: (layer, f, 0)),
                  pl.BlockSpec((1, d), lambda i, f: (0, 0))],
        out_specs=pl.BlockSpec((tm, d), lambda i, f: (i, 0)),
        out_shape=jax.ShapeDtypeStruct((m, d), F32),
        scratch_shapes=[pltpu.VMEM((tm, d), BF16)],
        compiler_params=_params("parallel", "arbitrary"),
        name="mlp",
    )(x, g, w1, w2, fg)


def _ssd_kernel(zx_ref, dt_ref, cw_ref, cb_ref, alog_ref, dsk_ref, nw_ref, e_ref, shift_ref, y_ref,
                ucat_ref, xc_ref, state_ref, *, d_inner):
    @pl.when(pl.program_id(1) == 0)
    def _():
        ucat_ref[0:CHUNK, :] = jnp.zeros((CHUNK, ucat_ref.shape[1]), BF16)
        state_ref[...] = jnp.zeros_like(state_ref)

    def chunk(ci, carry):
        rows = pl.ds(pl.multiple_of(ci * CHUNK, CHUNK), CHUNK)
        _ssd_chunk(zx_ref.at[0, rows], dt_ref.at[0, rows], cw_ref, cb_ref, alog_ref, dsk_ref, nw_ref,
                   e_ref, shift_ref, y_ref.at[0, rows], ucat_ref, xc_ref, state_ref, d_inner=d_inner)
        return carry

    lax.fori_loop(0, zx_ref.shape[1] // CHUNK, chunk, 0)


def _ssd_chunk(zx_ref, dt_ref, cw_ref, cb_ref, alog_ref, dsk_ref, nw_ref, e_ref, shift_ref, y_ref,
               ucat_ref, xc_ref, state_ref, *, d_inner):
    q = CHUNK
    conv_dim = d_inner + 2 * SSM_GROUPS * SSM_STATE
    gw = d_inner // SSM_GROUPS
    hpg = gw // SSM_HEAD_DIM

    cpiece = 512
    for c0 in range(0, conv_dim, cpiece):
        cs = slice(c0, c0 + cpiece)
        u = zx_ref[:, d_inner + c0:d_inner + c0 + cpiece]
        ucat_ref[q:2 * q, cs] = u
        back = _dot(shift_ref[...], ucat_ref[:, cs])
        acc = cb_ref[:, cs] + cw_ref[CONV_WIDTH - 1:CONV_WIDTH, cs] * u.astype(F32)
        for k in range(1, CONV_WIDTH):
            w_k = cw_ref[CONV_WIDTH - 1 - k:CONV_WIDTH - k, cs]
            acc = acc + w_k * back[(k - 1) * q:k * q]
        xc_ref[:, cs] = _silu(acc)
        ucat_ref[0:q, cs] = u

    dt = dt_ref[...]
    a = dt * (-jnp.exp(alog_ref[...]))
    row = lax.broadcasted_iota(jnp.int32, (q, q), 0)
    col = lax.broadcasted_iota(jnp.int32, (q, q), 1)
    causal = row >= col
    tril = causal.astype(BF16)
    a_hi, a_mid, a_lo = _split3(a)
    a_cs = _dot(tril, a_hi) + _dot(tril, a_mid) + _dot(tril, a_lo)
    a_cs_t = a_cs.T
    a_last = a_cs[q - 1:q, :]

    per_head = jnp.concatenate([_hi_lo(dt), _hi_lo(jnp.exp(a_cs)), _hi_lo(jnp.exp(a_last - a_cs))],
                               axis=0)

    lane = lax.broadcasted_iota(jnp.int32, (q, LANES), 1)
    lo_half = lane < SSM_HEAD_DIM

    for g in range(SSM_GROUPS):
        gs = slice(g * gw, (g + 1) * gw)
        bg = xc_ref[:, d_inner + g * SSM_STATE:d_inner + (g + 1) * SSM_STATE].astype(BF16)
        cg = xc_ref[:, d_inner + (SSM_GROUPS + g) * SSM_STATE:
                    d_inner + (SSM_GROUPS + g + 1) * SSM_STATE].astype(BF16)
        cb = _dot_nt(cg, bg)
        xs = xc_ref[:, gs]
        lanes_x = _dot(per_head, e_ref[:, gs])
        dt_x, grow_x, dec_x = lanes_x[0:q], lanes_x[q:2 * q], lanes_x[2 * q:3 * q]
        xdt = xs * dt_x
        st = state_ref[:, gs]
        y = _dot(cg, st.astype(BF16)) * grow_x
        new_st = _dot_tn(bg, (xdt * dec_x).astype(BF16))
        state_ref[:, gs] = st * grow_x[q - 1:q] + new_st

        ydiag = []
        for pr in range(hpg // 2):
            lhs = []
            for half in range(2):
                h = g * hpg + 2 * pr + half
                diff = a_cs[:, h:h + 1] - a_cs_t[h:h + 1, :]
                decay = jnp.exp(jnp.where(causal, diff, -jnp.inf))
                lhs.append((cb * decay).astype(BF16))
            xp = xdt[:, pr * LANES:(pr + 1) * LANES]
            rhs = jnp.concatenate([jnp.where(lo_half, xp, 0.0), jnp.where(lo_half, 0.0, xp)],
                                  axis=0).astype(BF16)
            ydiag.append(_dot(jnp.concatenate(lhs, axis=1), rhs))
        y = y + jnp.concatenate(ydiag, axis=1) + dsk_ref[:, gs] * xs

        gated = y * _silu(zx_ref[:, gs].astype(F32))
        yn = gated * lax.rsqrt(jnp.mean(gated * gated, axis=-1, keepdims=True) + EPS)
        y_ref[:, gs] = (yn * nw_ref[:, gs]).astype(y_ref.dtype)


def _ssd(zx, dt, conv_w, conv_b, a_log, d_skip_x, norm_w, expand, d_inner, chunks_per_step):
    bsz, s, n_zx = zx.shape
    conv_dim = n_zx - d_inner
    rows_per_step = chunks_per_step * CHUNK
    assert s % rows_per_step == 0
    kern = functools.partial(_ssd_kernel, d_inner=d_inner)
    const = lambda b, c: (0, 0)
    rows = jnp.arange((CONV_WIDTH - 1) * CHUNK, dtype=jnp.int32)
    src = CHUNK + rows % CHUNK - (rows // CHUNK + 1)
    shift = (jnp.arange(2 * CHUNK, dtype=jnp.int32)[None, :] == src[:, None]).astype(BF16)
    return pl.pallas_call(
        kern,
        grid=(bsz, s // rows_per_step),
        in_specs=[pl.BlockSpec((1, rows_per_step, n_zx), lambda b, c: (b, c, 0)),
                  pl.BlockSpec((1, rows_per_step, LANES), lambda b, c: (b, c, 0)),
                  pl.BlockSpec((CONV_WIDTH, conv_dim), const),
                  pl.BlockSpec((1, conv_dim), const),
                  pl.BlockSpec((1, LANES), const),
                  pl.BlockSpec((1, d_inner), const),
                  pl.BlockSpec((1, d_inner), const),
                  pl.BlockSpec((2 * LANES, d_inner), const),
                  pl.BlockSpec(((CONV_WIDTH - 1) * CHUNK, 2 * CHUNK), const)],
        out_specs=pl.BlockSpec((1, rows_per_step, d_inner), lambda b, c: (b, c, 0)),
        out_shape=jax.ShapeDtypeStruct((bsz, s, d_inner), BF16),
        scratch_shapes=[pltpu.VMEM((2 * CHUNK, conv_dim), BF16),
                        pltpu.VMEM((CHUNK, conv_dim), F32),
                        pltpu.VMEM((SSM_STATE, d_inner), F32)],
        compiler_params=_params("parallel", "arbitrary"),
        name="ssd_chunk",
    )(zx, dt, conv_w, conv_b, a_log, d_skip_x, norm_w, expand, shift)


def _attn_fused_kernel(slopes_ref, *refs, dilations, pairs):
    ng = len(dilations)
    ins, out_ref = refs[:5 * ng], refs[5 * ng]
    s_ref, p_ref, onat_ref, lnat_ref = refs[5 * ng + 1:]
    blk = ATTN_BLOCK
    sp, hb = pl.program_id(1), pl.program_id(2)
    nblk = out_ref.shape[1] // blk

    qi = lax.broadcasted_iota(jnp.int32, (blk, 2 * blk), 0)
    kj = lax.broadcasted_iota(jnp.int32, (blk, 2 * blk), 1)
    dist = qi + blk - kj
    band = (dist >= 0) & (dist <= blk)
    cur_only = band & (kj >= blk)
    negs = []
    for d in dilations:
        nd = -(dist * d).astype(F32)
        negs.append((jnp.where(band, nd, -jnp.inf), jnp.where(cur_only, nd, -jnp.inf)))
    lo_half = lax.broadcasted_iota(jnp.int32, (blk, LANES), 1) < ATTN_HEAD_DIM
    lo_half2 = lax.broadcasted_iota(jnp.int32, (2 * blk, LANES), 1) < ATTN_HEAD_DIM
    scale = 1.0 / math.sqrt(ATTN_HEAD_DIM)
    nil = jnp.zeros((2 * blk, LANES), BF16)
    qnil = jnp.zeros((blk, LANES), BF16)
    row4 = lax.broadcasted_iota(jnp.int32, (4 * blk, LANES), 0)
    lane4 = lax.broadcasted_iota(jnp.int32, (4 * blk, LANES), 1)
    ones_cat = jnp.where((row4 < 2 * blk) == (lane4 < ATTN_HEAD_DIM), 1.0, 0.0).astype(BF16)

    def history(i, d, cur_ref, prev_ref, ls):
        if d >= nblk:
            return prev_ref[0, pl.ds(pl.multiple_of(i * blk, blk), blk), ls]
        inside = cur_ref[0, pl.ds(pl.multiple_of(jnp.maximum(i - d, 0) * blk, blk), blk), ls]
        before = prev_ref[0, pl.ds(pl.multiple_of(jnp.minimum(i, d - 1) * blk, blk), blk), ls]
        return jnp.where(i >= d, inside, before)

    def block(i, carry):
        rows = pl.ds(pl.multiple_of(i * blk, blk), blk)
        for gi, d in enumerate(dilations):
            q_ref, k_ref, _, kp_ref, _ = ins[5 * gi:5 * gi + 5]
            has_prev = (i >= d) | (sp > 0)
            neg = jnp.where(has_prev, negs[gi][0], negs[gi][1])
            for pp in range(pairs):
                ls = slice(pp * LANES, (pp + 1) * LANES)
                u = gi * pairs + pp
                qp = q_ref[0, rows, ls] * scale
                q2 = jnp.concatenate([jnp.where(lo_half, qp, qnil), jnp.where(lo_half, qnil, qp)], axis=0)
                kk = jnp.concatenate([history(i, d, k_ref, kp_ref, ls), k_ref[0, rows, ls]], axis=0)
                s2 = _dot_nt(q2, kk)
                h0 = (hb * pairs + pp) * 2
                s_ref[u, 0:blk, :] = s2[0:blk] + slopes_ref[gi, h0] * neg
                s_ref[u, blk:2 * blk, :] = s2[blk:2 * blk] + slopes_ref[gi, h0 + 1] * neg
        ms = []
        for u in range(ng * pairs):
            s2 = s_ref[u]
            m = jnp.max(s2, axis=-1, keepdims=True)
            p = jnp.exp(s2 - m).astype(BF16)
            p_ref[u] = jnp.concatenate([p[0:blk], p[blk:2 * blk]], axis=1)
            ms.append(m)
        for gi, d in enumerate(dilations):
            _, _, v_ref, _, vp_ref = ins[5 * gi:5 * gi + 5]
            if d == 1:
                nat_rows = rows
            else:
                nat_rows = pl.ds((i // d) * (blk * d) + i % d, blk, stride=d)
            for pp in range(pairs):
                ls = slice(pp * LANES, (pp + 1) * LANES)
                u = gi * pairs + pp
                vv = jnp.concatenate([history(i, d, v_ref, vp_ref, ls), v_ref[0, rows, ls]], axis=0)
                vcat = jnp.concatenate([jnp.where(lo_half2, vv, nil), jnp.where(lo_half2, nil, vv)], axis=0)
                od = _dot(p_ref[u], jnp.concatenate([vcat, ones_cat], axis=1))
                den = od[:, LANES:]
                onat_ref[u, nat_rows, :] = od[:, :LANES] * (1.0 / den)
                lnat_ref[u, nat_rows, :] = (jnp.where(lo_half, ms[u][0:blk], ms[u][blk:2 * blk])
                                            + jnp.log(den))
        return carry

    lax.fori_loop(0, nblk, block, 0, unroll=4)

    def combine(c, carry):
        rows = pl.ds(pl.multiple_of(c * blk, blk), blk)
        for pp in range(pairs):
            ls_ = [lnat_ref[gi * pairs + pp, rows, :] for gi in range(ng)]
            m = functools.reduce(jnp.maximum, ls_)
            es = [jnp.exp(l - m) for l in ls_]
            mix = sum(e * onat_ref[gi * pairs + pp, rows, :] for gi, e in enumerate(es)) / sum(es)
            out_ref[0, rows, pp * LANES:(pp + 1) * LANES] = mix.astype(out_ref.dtype)
        return carry

    lax.fori_loop(0, nblk, combine, 0)


def _attn_fused(qkvs, slopes, bsz, s, dilations, pairs):
    w = HEADS_PER_GROUP * ATTN_HEAD_DIM
    cw = pairs * LANES
    ncb = w // cw
    span = ATTN_BLOCK * max(dilations)
    in_specs, args = [pl.BlockSpec(memory_space=pltpu.SMEM)], [slopes]
    for g, d in enumerate(dilations):
        for part in range(3):
            in_specs.append(pl.BlockSpec((1, span, cw),
                                         lambda b, sp, hb, part=part: (b, sp, part * ncb + hb)))
        hist_rows = ATTN_BLOCK * d
        per_span = span // hist_rows
        for part in (1, 2):
            in_specs.append(pl.BlockSpec(
                (1, hist_rows, cw),
                lambda b, sp, hb, part=part, ps=per_span: (b, jnp.maximum(sp * ps - 1, 0), part * ncb + hb)))
        args += [qkvs[g]] * 5
    units = len(dilations) * pairs
    return pl.pallas_call(
        functools.partial(_attn_fused_kernel, dilations=dilations, pairs=pairs),
        grid=(bsz, s // span, ncb),
        in_specs=in_specs,
        out_specs=pl.BlockSpec((1, span, cw), lambda b, sp, hb: (b, sp, hb)),
        out_shape=jax.ShapeDtypeStruct((bsz, s, w), BF16),
        scratch_shapes=[pltpu.VMEM((units, 2 * ATTN_BLOCK, 2 * ATTN_BLOCK), F32),
                        pltpu.VMEM((units, ATTN_BLOCK, 4 * ATTN_BLOCK), BF16),
                        pltpu.VMEM((units, span, LANES), F32),
                        pltpu.VMEM((units, span, LANES), F32)],
        compiler_params=_params("parallel", "parallel", "arbitrary"),
        name="dilated_attn_fused",
    )(*args)


def _alibi_slopes(n_groups):
    n = n_groups * HEADS_PER_GROUP
    i = jnp.arange(1, n + 1, dtype=F32)
    return jnp.exp2(-8.0 * i / n).reshape(n_groups, HEADS_PER_GROUP)


def _mamba_layer(x2, bsz, s, g_mix, w_in, conv_w, conv_b, dt_bias, a_log, d_skip, norm_w, w_out):
    d_model = x2.shape[1]
    d_inner = w_out.shape[0]
    heads = d_inner // SSM_HEAD_DIM
    n_zx = w_in.shape[1] - heads
    assert heads <= LANES and s % CHUNK == 0
    w_in = w_in.astype(BF16)
    w_dt = jnp.pad(w_in[:, n_zx:], ((0, 0), (0, LANES - heads)))
    dtb = jnp.pad(dt_bias, (0, LANES - heads)).reshape(1, LANES)
    alog = jnp.pad(a_log, (0, LANES - heads)).reshape(1, LANES)
    zx, dt = _inproj(x2, g_mix.reshape(1, d_model), w_in, n_zx, w_dt, dtb, tm=1024, tn=2560)

    lane_head = jnp.arange(d_inner, dtype=jnp.int32) // SSM_HEAD_DIM
    expand = (jnp.arange(LANES, dtype=jnp.int32)[:, None] == lane_head[None, :]).astype(BF16)
    expand = jnp.concatenate([expand, expand], axis=0)
    d_skip_x = jnp.repeat(d_skip, SSM_HEAD_DIM).reshape(1, d_inner)
    y = _ssd(zx.reshape(bsz, s, n_zx), dt.reshape(bsz, s, LANES), conv_w,
             conv_b.reshape(1, -1), alog, d_skip_x, norm_w.reshape(1, d_inner), expand, d_inner,
             chunks_per_step=4)
    return _matmul_res(y.reshape(bsz * s, d_inner), w_out.astype(BF16), x2, tm=1024, tn=1024)


def _attn_layer(x2, bsz, s, g_mix, w_qkv, w_o):
    d_model = x2.shape[1]
    n_groups = len(ATTN_GROUPS)
    assert all(win // dil == ATTN_BLOCK and s % win == 0 for win, dil in ATTN_GROUPS)
    dilations = tuple(dil for _, dil in ATTN_GROUPS)
    w = HEADS_PER_GROUP * ATTN_HEAD_DIM
    qkvs = _qkv_proj(x2, g_mix.reshape(1, d_model), w_qkv.astype(BF16), bsz, s, dilations,
                     tm=1024, tn=w)
    mix = _attn_fused(qkvs, _alibi_slopes(n_groups), bsz, s, dilations, pairs=2)
    return _matmul_res(mix.reshape(bsz * s, w), w_o.astype(BF16), x2, tm=1024, tn=2048)


def kernel(x, norm_mix, norm_mlp, ssm_w_in, ssm_conv_w, ssm_conv_b, ssm_dt_bias, ssm_a_log, ssm_d,
           ssm_norm_w, ssm_w_out, attn_w_qkv, attn_w_o, mlp_w1, mlp_w2, final_norm):
    bsz, s, d_model = x.shape
    depth = norm_mix.shape[0]
    x2 = x.reshape(bsz * s, d_model)
    fg = final_norm.reshape(1, d_model)
    w1, w2 = mlp_w1.astype(BF16), mlp_w2.astype(BF16)
    for i in range(depth):
        j = i // 2
        if i % 2 == 0:
            x2 = _mamba_layer(x2, bsz, s, norm_mix[i], ssm_w_in[j], ssm_conv_w[j], ssm_conv_b[j],
                              ssm_dt_bias[j], ssm_a_log[j], ssm_d[j], ssm_norm_w[j], ssm_w_out[j])
        else:
            x2 = _attn_layer(x2, bsz, s, norm_mix[i], attn_w_qkv[j], attn_w_o[j])
        x2 = _mlp(x2, norm_mlp[i].reshape(1, d_model), w1, w2, i, fg,
                  final_norm=(i == depth - 1), tm=1024, tf=1024)
    return x2.reshape(bsz, s, d_model)
```

```python
import functools
import math

import jax
import jax.numpy as jnp
from jax import lax
from jax.experimental import pallas as pl
from jax.experimental.pallas import tpu as pltpu

F32 = jnp.float32
BF16 = jnp.bfloat16

EPS = 1e-5
SSM_HEAD_DIM = 64
SSM_GROUPS = 8
SSM_STATE = 128
CONV_WIDTH = 4
CHUNK = 128
ATTN_GROUPS = ((128, 1), (512, 4), (2048, 16))
HEADS_PER_GROUP = 16
ATTN_HEAD_DIM = 64
ATTN_BLOCK = 128
LANES = 128

VMEM_LIMIT = 60 * 1024 * 1024


def _params(*sem):
    return pltpu.CompilerParams(dimension_semantics=sem, vmem_limit_bytes=VMEM_LIMIT)


def _rms(x, g):
    return x * lax.rsqrt(jnp.mean(x * x, axis=-1, keepdims=True) + EPS) * g


def _silu(x):
    h = 0.5 * x
    return h + h * jnp.tanh(h)


def _softplus(v):
    return jnp.maximum(v, 0.0) + jnp.log1p(jnp.exp(-jnp.abs(v)))


def _split3(v):
    hi = v.astype(BF16)
    r = v - hi.astype(F32)
    mid = r.astype(BF16)
    lo = (r - mid.astype(F32)).astype(BF16)
    return hi, mid, lo


def _dot(a, b):
    return jnp.dot(a, b, preferred_element_type=F32)


def _dot_nt(a, b):
    return lax.dot_general(a, b, (((1,), (1,)), ((), ())), preferred_element_type=F32)


def _dot_tn(a, b):
    return lax.dot_general(a, b, (((0,), (0,)), ((), ())), preferred_element_type=F32)


def _hi_lo(v):
    hi = v.astype(BF16)
    lo = (v - hi.astype(F32)).astype(BF16)
    return jnp.concatenate([hi, lo], axis=1)


def _perm_view_shape(bsz, s, dilation, tm, c):
    span = ATTN_BLOCK * dilation
    return (bsz, s // span, dilation, span // tm, tm // dilation, c)


def _perm_block_spec(s, dilation, tm, cblk, col_of):
    span = ATTN_BLOCK * dilation
    per_seq, per_span = s // tm, span // tm

    def index(i, *rest):
        ib = i % per_seq
        return (i // per_seq, ib // per_span, 0, ib % per_span, 0, col_of(*rest))

    return pl.BlockSpec((None, None, dilation, None, tm // dilation, cblk), index)


def _perm_pieces(tm, dilation):
    span = ATTN_BLOCK * dilation
    if tm >= span:
        return [(w0 + r * ATTN_BLOCK, w0 + r, ATTN_BLOCK)
                for w0 in range(0, tm, span) for r in range(dilation)]
    p = tm // dilation
    return [(r * p, r, p) for r in range(dilation)]


def _qkv_proj_kernel(x_ref, g_ref, w_ref, *refs, dilations):
    ng = len(dilations)
    o_refs, hf_ref, mid_ref, hp_ref = refs[:ng], refs[ng], refs[ng + 1], refs[ng + 2]
    j = pl.program_id(1)
    tm, k = x_ref.shape

    @pl.when(j == 0)
    def _():
        x = x_ref[...]
        inv = lax.rsqrt(jnp.mean(x * x, axis=-1, keepdims=True) + EPS)
        for c in range(k // LANES):
            cs = slice(c * LANES, (c + 1) * LANES)
            slot = c % hf_ref.shape[0]
            h = x_ref[:, cs] * inv * g_ref[:, cs]
            hf_ref[slot] = h
            for gi, d in enumerate(dilations):
                if d == 1:
                    hp_ref[gi, :, cs] = h.astype(BF16)
                elif d % 16 == 0 and tm < ATTN_BLOCK * d:
                    n, s2, p = tm // 4, d // 4, tm // d
                    for r in range(4):
                        mid_ref[slot, r * n:(r + 1) * n, :] = hf_ref[slot, pl.ds(r, n, stride=4), :]
                    for r in range(4):
                        for r2 in range(s2):
                            dst = (r + 4 * r2) * p
                            hp_ref[gi, dst:dst + p, cs] = (
                                mid_ref[slot, pl.ds(r * n + r2, p, stride=s2), :].astype(BF16))
                else:
                    for dst, src, p in _perm_pieces(tm, d):
                        hp_ref[gi, dst:dst + p, cs] = hf_ref[slot, pl.ds(src, p, stride=d), :].astype(BF16)

    res = _dot(hp_ref[j // 3], w_ref[...]).astype(BF16)
    for gi, d in enumerate(dilations):
        @pl.when(j // 3 == gi)
        def _(gi=gi, d=d):
            if len(o_refs[gi].shape) == 2:
                o_refs[gi][...] = res
            else:
                p = tm // d
                for r in range(d):
                    o_refs[gi][r] = res[r * p:(r + 1) * p, :]


def _qkv_proj(x, g, w, bsz, s, dilations, tm, tn):
    m, k = x.shape
    ng = len(dilations)
    wq = w.shape[1] // ng
    assert wq == 3 * tn and s % tm == 0
    out_specs, out_shapes = [], []
    for gi, d in enumerate(dilations):
        col_of = lambda j, gi=gi: jnp.clip(j - 3 * gi, 0, 2)
        if tm >= ATTN_BLOCK * d:
            out_specs.append(pl.BlockSpec((tm, tn), lambda i, j, c=col_of: (i, c(j))))
            out_shapes.append(jax.ShapeDtypeStruct((m, wq), BF16))
        else:
            out_specs.append(_perm_block_spec(s, d, tm, tn, col_of))
            out_shapes.append(jax.ShapeDtypeStruct(_perm_view_shape(bsz, s, d, tm, wq), BF16))
    outs = pl.pallas_call(
        functools.partial(_qkv_proj_kernel, dilations=dilations),
        grid=(m // tm, 3 * ng),
        in_specs=[pl.BlockSpec((tm, k), lambda i, j: (i, 0)),
                  pl.BlockSpec((1, k), lambda i, j: (0, 0)),
                  pl.BlockSpec((k, tn), lambda i, j: (0, (j % 3) * ng + j // 3))],
        out_specs=out_specs,
        out_shape=out_shapes,
        scratch_shapes=[pltpu.VMEM((2, tm, LANES), F32), pltpu.VMEM((2, tm, LANES), F32),
                        pltpu.VMEM((ng, tm, k), BF16)],
        compiler_params=_params("parallel", "arbitrary"),
        name="qkv_proj",
    )(x, g, w)
    return [o.reshape(bsz, s, wq) for o in outs]


def _inproj_kernel(x_ref, g_ref, w_ref, wdt_ref, dtb_ref, zx_ref, dt_ref, h_ref):
    @pl.when(pl.program_id(1) == 0)
    def _():
        h = _rms(x_ref[...], g_ref[...]).astype(BF16)
        h_ref[...] = h
        dt_ref[...] = _softplus(_dot(h, wdt_ref[...]) + dtb_ref[...])

    zx_ref[...] = _dot(h_ref[...], w_ref[...]).astype(zx_ref.dtype)


def _inproj(x, g, w_in, n, w_dt, dt_bias, tm, tn):
    m, k = x.shape
    assert n % tn == 0
    return pl.pallas_call(
        _inproj_kernel,
        grid=(m // tm, n // tn),
        in_specs=[pl.BlockSpec((tm, k), lambda i, j: (i, 0)),
                  pl.BlockSpec((1, k), lambda i, j: (0, 0)),
                  pl.BlockSpec((k, tn), lambda i, j: (0, j)),
                  pl.BlockSpec((k, LANES), lambda i, j: (0, 0)),
                  pl.BlockSpec((1, LANES), lambda i, j: (0, 0))],
        out_specs=[pl.BlockSpec((tm, tn), lambda i, j: (i, j)),
                   pl.BlockSpec((tm, LANES), lambda i, j: (i, 0))],
        out_shape=[jax.ShapeDtypeStruct((m, n), BF16),
                   jax.ShapeDtypeStruct((m, LANES), F32)],
        scratch_shapes=[pltpu.VMEM((tm, k), BF16)],
        compiler_params=_params("parallel", "arbitrary"),
        name="ssm_in_proj",
    )(x, g, w_in, w_dt, dt_bias)


def _matmul_res_kernel(a_ref, w_ref, r_ref, o_ref):
    o_ref[...] = r_ref[...] + _dot(a_ref[...], w_ref[...])


def _matmul_res(a, w, res, tm, tn):
    m, k = a.shape
    n = w.shape[1]
    return pl.pallas_call(
        _matmul_res_kernel,
        grid=(m // tm, n // tn),
        in_specs=[pl.BlockSpec((tm, k), lambda i, j: (i, 0)),
                  pl.BlockSpec((k, tn), lambda i, j: (0, j)),
                  pl.BlockSpec((tm, tn), lambda i, j: (i, j))],
        out_specs=pl.BlockSpec((tm, tn), lambda i, j: (i, j)),
        out_shape=jax.ShapeDtypeStruct((m, n), F32),
        compiler_params=_params("parallel", "arbitrary"),
        name="matmul_residual",
    )(a, w, res)


def _mlp_kernel(x_ref, g_ref, w1_ref, w2_ref, fg_ref, o_ref, h_ref, *, final_norm):
    f = pl.program_id(1)

    @pl.when(f == 0)
    def _():
        x = x_ref[...]
        h_ref[...] = _rms(x, g_ref[...]).astype(BF16)
        o_ref[...] = x

    a = jnp.maximum(_dot(h_ref[...], w1_ref[...]), 0.0)
    o_ref[...] += _dot((a * a).astype(BF16), w2_ref[...])

    if final_norm:
        @pl.when(f == pl.num_programs(1) - 1)
        def _():
            o_ref[...] = _rms(o_ref[...], fg_ref[...])


def _mlp(x, g, w1, w2, layer, fg, final_norm, tm, tf):
    m, d = x.shape
    ff = w1.shape[2]
    return pl.pallas_call(
        functools.partial(_mlp_kernel, final_norm=final_norm),
        grid=(m // tm, ff // tf),
        in_specs=[pl.BlockSpec((tm, d), lambda i, f: (i, 0)),
                  pl.BlockSpec((1, d), lambda i, f: (0, 0)),
                  pl.BlockSpec((None, d, tf), lambda i, f: (layer, 0, f)),
                  pl.BlockSpec((None, tf, d), lambda i, f: (layer, f, 0)),
                  pl.BlockSpec((1, d), lambda i, f: (0, 0))],
        out_specs=pl.BlockSpec((tm, d), lambda i, f: (i, 0)),
        out_shape=jax.ShapeDtypeStruct((m, d), F32),
        scratch_shapes=[pltpu.VMEM((tm, d), BF16)],
        compiler_params=_params("parallel", "arbitrary"),
        name="mlp",
    )(x, g, w1, w2, fg)


def _ssd_kernel(zx_ref, dt_ref, cw_ref, cb_ref, alog_ref, dsk_ref, nw_ref, e_ref, shift_ref, y_ref,
                ucat_ref, xc_ref, state_ref, *, d_inner):
    @pl.when(pl.program_id(1) == 0)
    def _():
        ucat_ref[0:CHUNK, :] = jnp.zeros((CHUNK, ucat_ref.shape[1]), BF16)
        state_ref[...] = jnp.zeros_like(state_ref)

    def chunk(ci, carry):
        rows = pl.ds(pl.multiple_of(ci * CHUNK, CHUNK), CHUNK)
        _ssd_chunk(zx_ref.at[0, rows], dt_ref.at[0, rows], cw_ref, cb_ref, alog_ref, dsk_ref, nw_ref,
                   e_ref, shift_ref, y_ref.at[0, rows], ucat_ref, xc_ref, state_ref, d_inner=d_inner)
        return carry

    lax.fori_loop(0, zx_ref.shape[1] // CHUNK, chunk, 0)


def _ssd_chunk(zx_ref, dt_ref, cw_ref, cb_ref, alog_ref, dsk_ref, nw_ref, e_ref, shift_ref, y_ref,
               ucat_ref, xc_ref, state_ref, *, d_inner):
    q = CHUNK
    conv_dim = d_inner + 2 * SSM_GROUPS * SSM_STATE
    gw = d_inner // SSM_GROUPS
    hpg = gw // SSM_HEAD_DIM

    cpiece = 512
    for c0 in range(0, conv_dim, cpiece):
        cs = slice(c0, c0 + cpiece)
        u = zx_ref[:, d_inner + c0:d_inner + c0 + cpiece]
        ucat_ref[q:2 * q, cs] = u
        back = _dot(shift_ref[...], ucat_ref[:, cs])
        acc = cb_ref[:, cs] + cw_ref[CONV_WIDTH - 1:CONV_WIDTH, cs] * u.astype(F32)
        for k in range(1, CONV_WIDTH):
            w_k = cw_ref[CONV_WIDTH - 1 - k:CONV_WIDTH - k, cs]
            acc = acc + w_k * back[(k - 1) * q:k * q]
        xc_ref[:, cs] = _silu(acc)
        ucat_ref[0:q, cs] = u

    dt = dt_ref[...]
    a = dt * (-jnp.exp(alog_ref[...]))
    row = lax.broadcasted_iota(jnp.int32, (q, q), 0)
    col = lax.broadcasted_iota(jnp.int32, (q, q), 1)
    causal = row >= col
    tril = causal.astype(BF16)
    a_hi, a_mid, a_lo = _split3(a)
    a_cs = _dot(tril, a_hi) + _dot(tril, a_mid) + _dot(tril, a_lo)
    a_cs_t = a_cs.T
    a_last = a_cs[q - 1:q, :]

    per_head = jnp.concatenate([_hi_lo(dt), _hi_lo(jnp.exp(a_cs)), _hi_lo(jnp.exp(a_last - a_cs))],
                               axis=0)

    lane = lax.broadcasted_iota(jnp.int32, (q, LANES), 1)
    lo_half = lane < SSM_HEAD_DIM

    for g in range(SSM_GROUPS):
        gs = slice(g * gw, (g + 1) * gw)
        bg = xc_ref[:, d_inner + g * SSM_STATE:d_inner + (g + 1) * SSM_STATE].astype(BF16)
        cg = xc_ref[:, d_inner + (SSM_GROUPS + g) * SSM_STATE:
                    d_inner + (SSM_GROUPS + g + 1) * SSM_STATE].astype(BF16)
        cb = _dot_nt(cg, bg)
        xs = xc_ref[:, gs]
        lanes_x = _dot(per_head, e_ref[:, gs])
        dt_x, grow_x, dec_x = lanes_x[0:q], lanes_x[q:2 * q], lanes_x[2 * q:3 * q]
        xdt = xs * dt_x
        st = state_ref[:, gs]
        y = _dot(cg, st.astype(BF16)) * grow_x
        new_st = _dot_tn(bg, (xdt * dec_x).astype(BF16))
        state_ref[:, gs] = st * grow_x[q - 1:q] + new_st

        ydiag = []
        for pr in range(hpg // 2):
            lhs = []
            for half in range(2):
                h = g * hpg + 2 * pr + half
                diff = a_cs[:, h:h + 1] - a_cs_t[h:h + 1, :]
                decay = jnp.exp(jnp.where(causal, diff, -jnp.inf))
                lhs.append((cb * decay).astype(BF16))
            xp = xdt[:, pr * LANES:(pr + 1) * LANES]
            rhs = jnp.concatenate([jnp.where(lo_half, xp, 0.0), jnp.where(lo_half, 0.0, xp)],
                                  axis=0).astype(BF16)
            ydiag.append(_dot(jnp.concatenate(lhs, axis=1), rhs))
        y = y + jnp.concatenate(ydiag, axis=1) + dsk_ref[:, gs] * xs

        gated = y * _silu(zx_ref[:, gs].astype(F32))
        yn = gated * lax.rsqrt(jnp.mean(gated * gated, axis=-1, keepdims=True) + EPS)
        y_ref[:, gs] = (yn * nw_ref[:, gs]).astype(y_ref.dtype)


def _ssd(zx, dt, conv_w, conv_b, a_log, d_skip_x, norm_w, expand, d_inner, chunks_per_step):
    bsz, s, n_zx = zx.shape
    conv_dim = n_zx - d_inner
    rows_per_step = chunks_per_step * CHUNK
    assert s % rows_per_step == 0
    kern = functools.partial(_ssd_kernel, d_inner=d_inner)
    const = lambda b, c: (0, 0)
    rows = jnp.arange((CONV_WIDTH - 1) * CHUNK, dtype=jnp.int32)
    src = CHUNK + rows % CHUNK - (rows // CHUNK + 1)
    shift = (jnp.arange(2 * CHUNK, dtype=jnp.int32)[None, :] == src[:, None]).astype(BF16)
    return pl.pallas_call(
        kern,
        grid=(bsz, s // rows_per_step),
        in_specs=[pl.BlockSpec((1, rows_per_step, n_zx), lambda b, c: (b, c, 0)),
                  pl.BlockSpec((1, rows_per_step, LANES), lambda b, c: (b, c, 0)),
                  pl.BlockSpec((CONV_WIDTH, conv_dim), const),
                  pl.BlockSpec((1, conv_dim), const),
                  pl.BlockSpec((1, LANES), const),
                  pl.BlockSpec((1, d_inner), const),
                  pl.BlockSpec((1, d_inner), const),
                  pl.BlockSpec((2 * LANES, d_inner), const),
                  pl.BlockSpec(((CONV_WIDTH - 1) * CHUNK, 2 * CHUNK), const)],
        out_specs=pl.BlockSpec((1, rows_per_step, d_inner), lambda b, c: (b, c, 0)),
        out_shape=jax.ShapeDtypeStruct((bsz, s, d_inner), BF16),
        scratch_shapes=[pltpu.VMEM((2 * CHUNK, conv_dim), BF16),
                        pltpu.VMEM((CHUNK, conv_dim), F32),
                        pltpu.VMEM((SSM_STATE, d_inner), F32)],
        compiler_params=_params("parallel", "arbitrary"),
        name="ssd_chunk",
    )(zx, dt, conv_w, conv_b, a_log, d_skip_x, norm_w, expand, shift)


def _attn_fused_kernel(slopes_ref, *refs, dilations, pairs):
    ng = len(dilations)
    ins, out_ref = refs[:5 * ng], refs[5 * ng]
    s_ref, p_ref, onat_ref, lnat_ref = refs[5 * ng + 1:]
    blk = ATTN_BLOCK
    sp, hb = pl.program_id(1), pl.program_id(2)
    nblk = out_ref.shape[1] // blk

    qi = lax.broadcasted_iota(jnp.int32, (blk, 2 * blk), 0)
    kj = lax.broadcasted_iota(jnp.int32, (blk, 2 * blk), 1)
    dist = qi + blk - kj
    band = (dist >= 0) & (dist <= blk)
    cur_only = band & (kj >= blk)
    negs = []
    for d in dilations:
        nd = -(dist * d).astype(F32)
        negs.append((jnp.where(band, nd, -jnp.inf), jnp.where(cur_only, nd, -jnp.inf)))
    lo_half = lax.broadcasted_iota(jnp.int32, (blk, LANES), 1) < ATTN_HEAD_DIM
    lo_half2 = lax.broadcasted_iota(jnp.int32, (2 * blk, LANES), 1) < ATTN_HEAD_DIM
    scale = 1.0 / math.sqrt(ATTN_HEAD_DIM)
    nil = jnp.zeros((2 * blk, LANES), BF16)
    qnil = jnp.zeros((blk, LANES), BF16)
    row4 = lax.broadcasted_iota(jnp.int32, (4 * blk, LANES), 0)
    lane4 = lax.broadcasted_iota(jnp.int32, (4 * blk, LANES), 1)
    ones_cat = jnp.where((row4 < 2 * blk) == (lane4 < ATTN_HEAD_DIM), 1.0, 0.0).astype(BF16)

    def history(i, d, cur_ref, prev_ref, ls):
        if d >= nblk:
            return prev_ref[0, pl.ds(pl.multiple_of(i * blk, blk), blk), ls]
        inside = cur_ref[0, pl.ds(pl.multiple_of(jnp.maximum(i - d, 0) * blk, blk), blk), ls]
        before = prev_ref[0, pl.ds(pl.multiple_of(jnp.minimum(i, d - 1) * blk, blk), blk), ls]
        return jnp.where(i >= d, inside, before)

    def block(i, carry):
        rows = pl.ds(pl.multiple_of(i * blk, blk), blk)
        for gi, d in enumerate(dilations):
            q_ref, k_ref, _, kp_ref, _ = ins[5 * gi:5 * gi + 5]
            has_prev = (i >= d) | (sp > 0)
            neg = jnp.where(has_prev, negs[gi][0], negs[gi][1])
            for pp in range(pairs):
                ls = slice(pp * LANES, (pp + 1) * LANES)
                u = gi * pairs + pp
                qp = q_ref[0, rows, ls] * scale
                q2 = jnp.concatenate([jnp.where(lo_half, qp, qnil), jnp.where(lo_half, qnil, qp)], axis=0)
                kk = jnp.concatenate([history(i, d, k_ref, kp_ref, ls), k_ref[0, rows, ls]], axis=0)
                s2 = _dot_nt(q2, kk)
                h0 = (hb * pairs + pp) * 2
                s_ref[u, 0:blk, :] = s2[0:blk] + slopes_ref[gi, h0] * neg
                s_ref[u, blk:2 * blk, :] = s2[blk:2 * blk] + slopes_ref[gi, h0 + 1] * neg
        ms = []
        for u in range(ng * pairs):
            s2 = s_ref[u]
            m = jnp.max(s2, axis=-1, keepdims=True)
            p = jnp.exp(s2 - m).astype(BF16)
            p_ref[u] = jnp.concatenate([p[0:blk], p[blk:2 * blk]], axis=1)
            ms.append(m)
        for gi, d in enumerate(dilations):
            _, _, v_ref, _, vp_ref = ins[5 * gi:5 * gi + 5]
            if d == 1:
                nat_rows = rows
            else:
                nat_rows = pl.ds((i // d) * (blk * d) + i % d, blk, stride=d)
            for pp in range(pairs):
                ls = slice(pp * LANES, (pp + 1) * LANES)
                u = gi * pairs + pp
                vv = jnp.concatenate([history(i, d, v_ref, vp_ref, ls), v_ref[0, rows, ls]], axis=0)
                vcat = jnp.concatenate([jnp.where(lo_half2, vv, nil), jnp.where(lo_half2, nil, vv)], axis=0)
                od = _dot(p_ref[u], jnp.concatenate([vcat, ones_cat], axis=1))
                den = od[:, LANES:]
                onat_ref[u, nat_rows, :] = od[:, :LANES] * (1.0 / den)
                lnat_ref[u, nat_rows, :] = (jnp.where(lo_half, ms[u][0:blk], ms[u][blk:2 * blk])
                                            + jnp.log(den))
        return carry

    lax.fori_loop(0, nblk, block, 0, unroll=8)

    def combine(c, carry):
        rows = pl.ds(pl.multiple_of(c * blk, blk), blk)
        for pp in range(pairs):
            ls_ = [lnat_ref[gi * pairs + pp, rows, :] for gi in range(ng)]
            m = functools.reduce(jnp.maximum, ls_)
            es = [jnp.exp(l - m) for l in ls_]
            mix = sum(e * onat_ref[gi * pairs + pp, rows, :] for gi, e in enumerate(es)) / sum(es)
            out_ref[0, rows, pp * LANES:(pp + 1) * LANES] = mix.astype(out_ref.dtype)
        return carry

    lax.fori_loop(0, nblk, combine, 0)


def _attn_fused(qkvs, slopes, bsz, s, dilations, pairs):
    w = HEADS_PER_GROUP * ATTN_HEAD_DIM
    cw = pairs * LANES
    ncb = w // cw
    span = ATTN_BLOCK * max(dilations)
    in_specs, args = [pl.BlockSpec(memory_space=pltpu.SMEM)], [slopes]
    for g, d in enumerate(dilations):
        for part in range(3):
            in_specs.append(pl.BlockSpec((1, span, cw),
                                         lambda b, sp, hb, part=part: (b, sp, part * ncb + hb)))
        hist_rows = ATTN_BLOCK * d
        per_span = span // hist_rows
        for part in (1, 2):
            in_specs.append(pl.BlockSpec(
                (1, hist_rows, cw),
                lambda b, sp, hb, part=part, ps=per_span: (b, jnp.maximum(sp * ps - 1, 0), part * ncb + hb)))
        args += [qkvs[g]] * 5
    units = len(dilations) * pairs
    return pl.pallas_call(
        functools.partial(_attn_fused_kernel, dilations=dilations, pairs=pairs),
        grid=(bsz, s // span, ncb),
        in_specs=in_specs,
        out_specs=pl.BlockSpec((1, span, cw), lambda b, sp, hb: (b, sp, hb)),
        out_shape=jax.ShapeDtypeStruct((bsz, s, w), BF16),
        scratch_shapes=[pltpu.VMEM((units, 2 * ATTN_BLOCK, 2 * ATTN_BLOCK), F32),
                        pltpu.VMEM((units, ATTN_BLOCK, 4 * ATTN_BLOCK), BF16),
                        pltpu.VMEM((units, span, LANES), F32),
                        pltpu.VMEM((units, span, LANES), F32)],
        compiler_params=_params("parallel", "parallel", "arbitrary"),
        name="dilated_attn_fused",
    )(*args)


def _alibi_slopes(n_groups):
    n = n_groups * HEADS_PER_GROUP
    i = jnp.arange(1, n + 1, dtype=F32)
    return jnp.exp2(-8.0 * i / n).reshape(n_groups, HEADS_PER_GROUP)


def _mamba_layer(x2, bsz, s, g_mix, w_in, conv_w, conv_b, dt_bias, a_log, d_skip, norm_w, w_out):
    d_model = x2.shape[1]
    d_inner = w_out.shape[0]
    heads = d_inner // SSM_HEAD_DIM
    n_zx = w_in.shape[1] - heads
    assert heads <= LANES and s % CHUNK == 0
    w_in = w_in.astype(BF16)
    w_dt = jnp.pad(w_in[:, n_zx:], ((0, 0), (0, LANES - heads)))
    dtb = jnp.pad(dt_bias, (0, LANES - heads)).reshape(1, LANES)
    alog = jnp.pad(a_log, (0, LANES - heads)).reshape(1, LANES)
    zx, dt = _inproj(x2, g_mix.reshape(1, d_model), w_in, n_zx, w_dt, dtb, tm=1024, tn=2560)

    lane_head = jnp.arange(d_inner, dtype=jnp.int32) // SSM_HEAD_DIM
    expand = (jnp.arange(LANES, dtype=jnp.int32)[:, None] == lane_head[None, :]).astype(BF16)
    expand = jnp.concatenate([expand, expand], axis=0)
    d_skip_x = jnp.repeat(d_skip, SSM_HEAD_DIM).reshape(1, d_inner)
    y = _ssd(zx.reshape(bsz, s, n_zx), dt.reshape(bsz, s, LANES), conv_w,
             conv_b.reshape(1, -1), alog, d_skip_x, norm_w.reshape(1, d_inner), expand, d_inner,
             chunks_per_step=4)
    return _matmul_res(y.reshape(bsz * s, d_inner), w_out.astype(BF16), x2, tm=1024, tn=1024)


def _attn_layer(x2, bsz, s, g_mix, w_qkv, w_o):
    d_model = x2.shape[1]
    n_groups = len(ATTN_GROUPS)
    assert all(win // dil == ATTN_BLOCK and s % win == 0 for win, dil in ATTN_GROUPS)
    dilations = tuple(dil for _, dil in ATTN_GROUPS)
    w = HEADS_PER_GROUP * ATTN_HEAD_DIM
    qkvs = _qkv_proj(x2, g_mix.reshape(1, d_model), w_qkv.astype(BF16), bsz, s, dilations,
                     tm=1024, tn=w)
    mix = _attn_fused(qkvs, _alibi_slopes(n_groups), bsz, s, dilations, pairs=2)
    return _matmul_res(mix.reshape(bsz * s, w), w_o.astype(BF16), x2, tm=1024, tn=2048)


def kernel(x, norm_mix, norm_mlp, ssm_w_in, ssm_conv_w, ssm_conv_b, ssm_dt_bias, ssm_a_log, ssm_d,
           ssm_norm_w, ssm_w_out, attn_w_qkv, attn_w_o, mlp_w1, mlp_w2, final_norm):
    bsz, s, d_model = x.shape
    depth = norm_mix.shape[0]
    x2 = x.reshape(bsz * s, d_model)
    fg = final_norm.reshape(1, d_model)
    w1, w2 = mlp_w1.astype(BF16), mlp_w2.astype(BF16)
    for i in range(depth):
        j = i // 2
        if i % 2 == 0:
            x2 = _mamba_layer(x2, bsz, s, norm_mix[i], ssm_w_in[j], ssm_conv_w[j], ssm_conv_b[j],
                              ssm_dt_bias[j], ssm_a_log[j], ssm_d[j], ssm_norm_w[j], ssm_w_out[j])
        else:
            x2 = _attn_layer(x2, bsz, s, norm_mix[i], attn_w_qkv[j], attn_w_o[j])
        x2 = _mlp(x2, norm_mlp[i].reshape(1, d_model), w1, w2, i, fg,
                  final_norm=(i == depth - 1), tm=1024, tf=1024)
    return x2.reshape(bsz, s, d_model)
```
